```python
import jax, jax.numpy as jnp
from jax import lax
import numpy as np

D_MODEL = 2048
BATCH = 2
SEQ = 8192
DEPTH = 1

MLA_HEADS = 16
Q_LORA = 512
KV_LORA = 512
QK_NOPE = 128
QK_ROPE = 64
V_HEAD = 128
QK_HEAD = QK_NOPE + QK_ROPE
ROPE_THETA = 10000.0
Q_BLOCK = 128
SGU_WIDTH = D_MODEL
SGU_GROUPS = 8
SGU_GROUP_DIM = SGU_WIDTH // SGU_GROUPS
CHUNK = 128
PEER_HEADS = 8
N_KEYS = 128
N_EXPERTS = N_KEYS * N_KEYS
PEER_TOPK = 16
KEY_DIM = 128
TOKEN_BLOCK = 128
EPS = 1e-6

OFF_CQ = Q_LORA
OFF_CKV = OFF_CQ + KV_LORA
OFF_KPE = OFF_CKV + QK_ROPE
OFF_SGU = OFF_KPE + 2 * SGU_WIDTH
IN_COLS = OFF_SGU + 2 * D_MODEL

kernel_name = "hybrid_mla_sgu_peer_block"


def rmsnorm(x, g):
    xf = x.astype(jnp.float32)
    r = lax.rsqrt(jnp.mean(xf * xf, axis=-1, keepdims=True) + EPS)
    return (xf * r).astype(x.dtype) * g


def layernorm(x, g, b):
    xf = x.astype(jnp.float32)
    mu = jnp.mean(xf, axis=-1, keepdims=True)
    var = jnp.mean(jnp.square(xf - mu), axis=-1, keepdims=True)
    return ((xf - mu) * lax.rsqrt(var + EPS)).astype(x.dtype) * g + b


def rope(x, positions):
    d = x.shape[-1]
    freqs = ROPE_THETA ** (-jnp.arange(0, d, 2, dtype=jnp.float32) / d)
    ang = positions.astype(jnp.float32)[..., None] * freqs
    cos = jnp.cos(ang)[:, :, None, :].astype(x.dtype)
    sin = jnp.sin(ang)[:, :, None, :].astype(x.dtype)
    x1, x2 = x[..., : d // 2], x[..., d // 2 :]
    return jnp.concatenate([x1 * cos - x2 * sin, x1 * sin + x2 * cos], axis=-1)


def causal_block_attention(q, k, v):
    B, S, H, Dq = q.shape
    Dv = v.shape[-1]
    nb = S // Q_BLOCK
    qb = q.reshape(B, nb, Q_BLOCK, H, Dq).transpose(1, 0, 2, 3, 4)
    key_pos = jnp.arange(S)
    scale = Dq ** -0.5

    def one_block(args):
        q_blk, blk = args
        s = jnp.einsum("bqhd,bkhd->bhqk", q_blk, k).astype(jnp.float32) * scale
        q_pos = blk * Q_BLOCK + jnp.arange(Q_BLOCK)
        mask = q_pos[:, None] >= key_pos[None, :]
        s = jnp.where(mask[None, None], s, -jnp.inf)
        p = jax.nn.softmax(s, axis=-1).astype(v.dtype)
        return jnp.einsum("bhqk,bkhd->bqhd", p, v)

    o = lax.map(one_block, (qb, jnp.arange(nb)))
    return o.transpose(1, 0, 2, 3, 4).reshape(B, S, H * Dv)


def mla_branch(c_q, c_kv, k_pe, positions, g_q_a, w_uq, g_kv_a, w_ukv):
    B, S, _ = c_q.shape
    q = jnp.einsum("bsr,rhd->bshd", rmsnorm(c_q, g_q_a), w_uq)
    q = jnp.concatenate([q[..., :QK_NOPE], rope(q[..., QK_NOPE:], positions)], axis=-1)
    kv = jnp.einsum("bsr,rhd->bshd", rmsnorm(c_kv, g_kv_a), w_ukv)
    k_nope, v = kv[..., :QK_NOPE], kv[..., QK_NOPE:]
    k_pe = rope(k_pe[:, :, None, :], positions)
    k = jnp.concatenate([k_nope, jnp.broadcast_to(k_pe, (B, S, MLA_HEADS, QK_ROPE))], axis=-1)
    return causal_block_attention(q, k, v)


def sgu_branch(z, ln_g, ln_b, w_spatial, b_spatial):
    z = jax.nn.gelu(z)
    u, v = z[..., :SGU_WIDTH], z[..., SGU_WIDTH:]
    v = layernorm(v, ln_g, ln_b)
    B, S, _ = v.shape
    vg = v.reshape(B, S // CHUNK, CHUNK, SGU_GROUPS, SGU_GROUP_DIM)
    w = w_spatial * jnp.tril(jnp.ones((CHUNK, CHUNK), dtype=w_spatial.dtype))[None]
    mixed = jnp.einsum("gts,bcsgd->bctgd", w, vg) + b_spatial.T[None, None, :, :, None]
    return u * mixed.reshape(B, S, SGU_WIDTH)


def peer_ffn(h, w_pq, peer_keys, peer_u, peer_v):
    B, S, D = h.shape
    q = jnp.einsum("bsd,dhk->bshk", h, w_pq).reshape(B, S, PEER_HEADS, 2, KEY_DIM)
    sub = jnp.einsum("bshpk,hpnk->bshpn", q, peer_keys).astype(jnp.float32)
    s, i = lax.top_k(sub, PEER_TOPK)
    cand = (s[..., 0, :, None] + s[..., 1, None, :]).reshape(B, S, PEER_HEADS, PEER_TOPK * PEER_TOPK)
    cidx = (i[..., 0, :, None] * N_KEYS + i[..., 1, None, :]).reshape(B, S, PEER_HEADS, PEER_TOPK * PEER_TOPK)
    top_s, pos = lax.top_k(cand, PEER_TOPK)
    idx = jnp.take_along_axis(cidx, pos, axis=-1)
    gate = jax.nn.softmax(top_s, axis=-1).astype(h.dtype)
    nt = (B * S) // TOKEN_BLOCK
    hb = h.reshape(nt, TOKEN_BLOCK, D)
    ib = idx.reshape(nt, TOKEN_BLOCK, PEER_HEADS, PEER_TOPK)
    gb = gate.reshape(nt, TOKEN_BLOCK, PEER_HEADS, PEER_TOPK)

    def one_block(args):
        x_blk, i_blk, g_blk = args
        u = jnp.take(peer_u, i_blk, axis=0)
        a = jax.nn.gelu(jnp.einsum("td,thkd->thk", x_blk, u)) * g_blk
        v = jnp.take(peer_v, i_blk, axis=0)
        return jnp.einsum("thk,thkd->td", a, v)

    return lax.map(one_block, (hb, ib, gb)).reshape(B, S, D)


def setup_inputs(seed: int = 0) -> dict:
    key = jax.random.key(seed)
    ks = jax.random.split(key, 24)
    f32 = jnp.float32
    nrm = lambda k, shape, scale: jax.random.normal(k, shape, f32) * scale
    gain = lambda k, shape: 1.0 + 0.05 * jax.random.normal(k, shape, f32)
    L = DEPTH
    x = jax.random.normal(ks[0], (BATCH, SEQ, D_MODEL), f32)
    offsets = jax.random.randint(ks[1], (BATCH, 1), 0, 4096, dtype=jnp.int32)
    positions = offsets + jnp.arange(SEQ, dtype=jnp.int32)[None, :]
    return {
        "x": x,
        "positions": positions,
        "g_norm1": gain(ks[2], (L, D_MODEL)),
        "w_in": nrm(ks[3], (L, D_MODEL, IN_COLS), D_MODEL ** -0.5),
        "g_q_a": gain(ks[4], (L, Q_LORA)),
        "w_uq": nrm(ks[5], (L, Q_LORA, MLA_HEADS, QK_HEAD), Q_LORA ** -0.5),
        "g_kv_a": gain(ks[6], (L, KV_LORA)),
        "w_ukv": nrm(ks[7], (L, KV_LORA, MLA_HEADS, QK_NOPE + V_HEAD), KV_LORA ** -0.5),
        "sgu_ln_g": gain(ks[8], (L, SGU_WIDTH)),
        "sgu_ln_b": nrm(ks[9], (L, SGU_WIDTH), 0.02),
        "w_spatial": nrm(ks[10], (L, SGU_GROUPS, CHUNK, CHUNK), CHUNK ** -0.5),
        "b_spatial": gain(ks[11], (L, SGU_GROUPS, CHUNK)),
        "b_gate": nrm(ks[12], (L, 2 * D_MODEL), 0.01),
        "w_out": nrm(ks[13], (L, D_MODEL, D_MODEL), D_MODEL ** -0.5),
        "g_norm2": gain(ks[14], (L, D_MODEL)),
        "w_peer_q": nrm(ks[15], (L, D_MODEL, PEER_HEADS, 2 * KEY_DIM), D_MODEL ** -0.5),
        "peer_keys": nrm(ks[16], (L, PEER_HEADS, 2, N_KEYS, KEY_DIM), KEY_DIM ** -0.5),
        "peer_u": nrm(ks[17], (L, N_EXPERTS, D_MODEL), D_MODEL ** -0.5),
        "peer_v": nrm(ks[18], (L, N_EXPERTS, D_MODEL), PEER_HEADS ** -0.5),
        "g_final": gain(ks[19], (D_MODEL,)),
    }


def reference(x, positions, g_norm1, w_in, g_q_a, w_uq, g_kv_a, w_ukv, sgu_ln_g, sgu_ln_b,
              w_spatial, b_spatial, b_gate, w_out, g_norm2, w_peer_q, peer_keys, peer_u,
              peer_v, g_final):
    for l in range(DEPTH):
        h = rmsnorm(x, g_norm1[l])
        proj = h @ w_in[l]
        c_q = proj[..., :OFF_CQ]
        c_kv = proj[..., OFF_CQ:OFF_CKV]
        k_pe = proj[..., OFF_CKV:OFF_KPE]
        z_sgu = proj[..., OFF_KPE:OFF_SGU]
        gates = jax.nn.sigmoid(proj[..., OFF_SGU:] + b_gate[l])
        o_a = mla_branch(c_q, c_kv, k_pe, positions, g_q_a[l], w_uq[l], g_kv_a[l], w_ukv[l])
        o_b = sgu_branch(z_sgu, sgu_ln_g[l], sgu_ln_b[l], w_spatial[l], b_spatial[l])
        merged = gates[..., :D_MODEL] * o_a + gates[..., D_MODEL:] * o_b
        x = x + merged @ w_out[l]
        h2 = rmsnorm(x, g_norm2[l])
        x = x + peer_ffn(h2, w_peer_q[l], peer_keys[l], peer_u[l], peer_v[l])
    return rmsnorm(x, g_final)
```

```python
import functools
import math

import jax
import jax.numpy as jnp
from jax import lax
from jax.experimental import pallas as pl
from jax.experimental.pallas import tpu as pltpu

BF16 = jnp.bfloat16
F32 = jnp.float32

EPS = 1e-6
ROPE_THETA = 10000.0
QK_NOPE = 128
QK_ROPE = 64
V_HEAD = 128
QK_PAD = 256
CHUNK = 128
N_KEYS = 128
PEER_TOPK = 16
LANE = 128
VMEM_LIMIT_BYTES = 56 * 1024 * 1024
NEG_INF = float("-inf")


def _params(*sem):
    return pltpu.CompilerParams(dimension_semantics=sem, vmem_limit_bytes=VMEM_LIMIT_BYTES)


def _tile(n, pref):
    t = min(n, pref)
    assert n % t == 0, (n, t)
    return t


def _gelu_tanh(x):
    c = math.sqrt(2.0 / math.pi)
    return x * (0.5 * (1.0 + jnp.tanh(c * (x + 0.044715 * (x * x * x)))))


def _rms(x, g):
    r = lax.rsqrt(jnp.mean(x * x, axis=-1, keepdims=True) + EPS)
    return (x * r) * g


def _norm_kernel(x_ref, g_ref, o_ref):
    o_ref[...] = _rms(x_ref[...], g_ref[...]).astype(o_ref.dtype)


def _norm_call(x, g):
    t, d = x.shape
    tm = _tile(t, 512)
    return pl.pallas_call(
        _norm_kernel,
        grid=(t // tm,),
        in_specs=[pl.BlockSpec((tm, d), lambda i: (i, 0)), pl.BlockSpec((1, d), lambda i: (0, 0))],
        out_specs=pl.BlockSpec((tm, d), lambda i: (i, 0)),
        out_shape=jax.ShapeDtypeStruct((t, d), BF16),
        compiler_params=_params("parallel"),
        name="norm1",
    )(x, g)


def _latent_kernel(h_ref, w_ref, gq_ref, gkv_ref, pos_ref, freq_ref, sign_ref,
                   cq_ref, ckv_ref, kpe_ref, cos_ref, sin_ref, *, q_lora, kv_lora):
    p = jnp.dot(h_ref[...], w_ref[...], preferred_element_type=F32)
    o = q_lora + kv_lora
    cq_ref[...] = _rms(p[:, :q_lora], gq_ref[...]).astype(cq_ref.dtype)
    ckv_ref[...] = _rms(p[:, q_lora:o], gkv_ref[...]).astype(ckv_ref.dtype)
    ang = pos_ref[...] * freq_ref[...]
    c = jnp.cos(ang)
    s = jnp.sin(ang) * sign_ref[...]
    cos_ref[...] = c
    sin_ref[...] = s
    kr = p[:, o:o + QK_ROPE] * c + p[:, o + QK_ROPE:o + 2 * QK_ROPE] * s
    kpe_ref[...] = jnp.concatenate([kr, jnp.zeros_like(kr)], axis=-1).astype(kpe_ref.dtype)


def _latent_call(h, w_lat, gq, gkv, pos, freq, sign):
    t, d = h.shape
    q_lora, kv_lora = gq.shape[1], gkv.shape[1]
    n = w_lat.shape[1]
    tm = _tile(t, 512)
    row = lambda i: (i, 0)
    const = lambda i: (0, 0)
    return pl.pallas_call(
        functools.partial(_latent_kernel, q_lora=q_lora, kv_lora=kv_lora),
        grid=(t // tm,),
        in_specs=[
            pl.BlockSpec((tm, d), row), pl.BlockSpec((d, n), const),
            pl.BlockSpec((1, q_lora), const), pl.BlockSpec((1, kv_lora), const),
            pl.BlockSpec((tm, 1), row), pl.BlockSpec((1, QK_ROPE), const), pl.BlockSpec((1, QK_ROPE), const),
        ],
        out_specs=[
            pl.BlockSpec((tm, q_lora), row), pl.BlockSpec((tm, kv_lora), row),
            pl.BlockSpec((tm, 2 * QK_ROPE), row), pl.BlockSpec((tm, QK_ROPE), row), pl.BlockSpec((tm, QK_ROPE), row),
        ],
        out_shape=[
            jax.ShapeDtypeStruct((t, q_lora), BF16), jax.ShapeDtypeStruct((t, kv_lora), BF16),
            jax.ShapeDtypeStruct((t, 2 * QK_ROPE), BF16),
            jax.ShapeDtypeStruct((t, QK_ROPE), F32), jax.ShapeDtypeStruct((t, QK_ROPE), F32),
        ],
        compiler_params=_params("parallel"),
        name="latent",
    )(h, w_lat, gq, gkv, pos, freq, sign)


def _q_kernel(c_ref, w_ref, cos_ref, sin_ref, q_ref, *, scale):
    r = jnp.dot(c_ref[...], w_ref[0], preferred_element_type=F32)
    qr = r[:, QK_NOPE:QK_NOPE + QK_ROPE] * cos_ref[...] + r[:, QK_NOPE + QK_ROPE:] * sin_ref[...]
    q = jnp.concatenate([r[:, :QK_NOPE], qr, jnp.zeros_like(qr)], axis=-1) * scale
    q_ref[0, 0] = q.astype(q_ref.dtype)


def _kv_kernel(c_ref, w_ref, kpe_ref, k_ref, v_ref):
    r = jnp.dot(c_ref[...], w_ref[0], preferred_element_type=F32)
    k_ref[0, 0] = jnp.concatenate([r[:, :QK_NOPE].astype(k_ref.dtype), kpe_ref[...]], axis=-1)
    v_ref[0, 0] = r[:, QK_NOPE:].astype(v_ref.dtype)


def _head_proj_calls(cq, ckv, wq, wkv, kpe, cos, sin, batch, seq, scale):
    t = cq.shape[0]
    heads = wq.shape[0]
    tm = _tile(seq, 512)
    spb = seq // tm
    row = lambda i, h: (i, 0)
    whead = lambda i, h: (h, 0, 0)
    out = lambda i, h: (i // spb, h, i % spb, 0)
    q = pl.pallas_call(
        functools.partial(_q_kernel, scale=scale),
        grid=(t // tm, heads),
        in_specs=[pl.BlockSpec((tm, cq.shape[1]), row), pl.BlockSpec((1,) + wq.shape[1:], whead),
                  pl.BlockSpec((tm, QK_ROPE), row), pl.BlockSpec((tm, QK_ROPE), row)],
        out_specs=pl.BlockSpec((1, 1, tm, QK_PAD), out),
        out_shape=jax.ShapeDtypeStruct((batch, heads, seq, QK_PAD), BF16),
        compiler_params=_params("parallel", "arbitrary"),
        name="q_proj",
    )(cq, wq, cos, sin)
    k, v = pl.pallas_call(
        _kv_kernel,
        grid=(t // tm, heads),
        in_specs=[pl.BlockSpec((tm, ckv.shape[1]), row), pl.BlockSpec((1,) + wkv.shape[1:], whead),
                  pl.BlockSpec((tm, 2 * QK_ROPE), row)],
        out_specs=[pl.BlockSpec((1, 1, tm, QK_PAD), out), pl.BlockSpec((1, 1, tm, V_HEAD), out)],
        out_shape=[jax.ShapeDtypeStruct((batch, heads, seq, QK_PAD), BF16),
                   jax.ShapeDtypeStruct((batch, heads, seq, V_HEAD), BF16)],
        compiler_params=_params("parallel", "arbitrary"),
        name="kv_proj",
    )(ckv, wkv, kpe)
    return q, k, v


def _flash_kernel(q_ref, k_ref, v_ref, o_ref, m_ref, l_ref, acc_ref):
    qi = pl.program_id(2)
    ki = pl.program_id(3)

    @pl.when(ki == 0)
    def _():
        m_ref[...] = jnp.full_like(m_ref, NEG_INF)
        l_ref[...] = jnp.zeros_like(l_ref)
        acc_ref[...] = jnp.zeros_like(acc_ref)

    def step(diagonal):
        s = lax.dot_general(q_ref[0, 0], k_ref[0, 0], (((1,), (1,)), ((), ())),
                            preferred_element_type=F32)
        if diagonal:
            row = lax.broadcasted_iota(jnp.int32, s.shape, 0)
            col = lax.broadcasted_iota(jnp.int32, s.shape, 1)
            s = jnp.where(row >= col, s, NEG_INF)
        m_prev = m_ref[...]
        m_new = jnp.maximum(m_prev, jnp.max(s, axis=-1, keepdims=True))
        alpha = jnp.exp2(m_prev - m_new)
        p = jnp.exp2(s - m_new)
        l_ref[...] = alpha * l_ref[...] + jnp.sum(p, axis=-1, keepdims=True)
        acc_ref[...] = alpha * acc_ref[...] + jnp.dot(p.astype(v_ref.dtype), v_ref[0, 0],
                                                      preferred_element_type=F32)
        m_ref[...] = m_new

    @pl.when(ki < qi)
    def _():
        step(False)

    @pl.when(ki == qi)
    def _():
        step(True)
        o_ref[0] = (acc_ref[...] / l_ref[...]).astype(o_ref.dtype)


def _flash_call(q, k, v):
    batch, heads, seq, _ = q.shape
    tq = _tile(seq, 512)
    nq = seq // tq
    qmap = lambda b, h, i, j: (b, h, i, 0)
    kmap = lambda b, h, i, j: (b, h, jnp.minimum(i, j), 0)
    return pl.pallas_call(
        _flash_kernel,
        grid=(batch, heads, nq, nq),
        in_specs=[pl.BlockSpec((1, 1, tq, QK_PAD), qmap), pl.BlockSpec((1, 1, tq, QK_PAD), kmap),
                  pl.BlockSpec((1, 1, tq, V_HEAD), kmap)],
        out_specs=pl.BlockSpec((1, tq, V_HEAD), lambda b, h, i, j: (b, i, h)),
        out_shape=jax.ShapeDtypeStruct((batch, seq, heads * V_HEAD), BF16),
        scratch_shapes=[pltpu.VMEM((tq, 1), F32), pltpu.VMEM((tq, 1), F32), pltpu.VMEM((tq, V_HEAD), F32)],
        compiler_params=_params("parallel", "parallel", "parallel", "arbitrary"),
        name="flash_attn",
    )(q, k, v)


def _zg_kernel(h_ref, w_ref, b_ref, o_ref, *, n_gelu_blocks):
    j = pl.program_id(0)
    z = jnp.dot(h_ref[...], w_ref[...], preferred_element_type=F32)

    @pl.when(j < n_gelu_blocks)
    def _():
        o_ref[...] = _gelu_tanh(z).astype(o_ref.dtype)

    @pl.when(j >= n_gelu_blocks)
    def _():
        o_ref[...] = (1.0 / (1.0 + jnp.exp(-(z + b_ref[...])))).astype(o_ref.dtype)


def _zg_call(h, w_zg, bias, n_gelu_cols):
    t, d = h.shape
    n = w_zg.shape[1]
    tm = _tile(t, 512)
    tn = _tile(n_gelu_cols, 1024)
    return pl.pallas_call(
        functools.partial(_zg_kernel, n_gelu_blocks=n_gelu_cols // tn),
        grid=(n // tn, t // tm),
        in_specs=[pl.BlockSpec((tm, d), lambda j, i: (i, 0)), pl.BlockSpec((d, tn), lambda j, i: (0, j)),
                  pl.BlockSpec((1, tn), lambda j, i: (0, j))],
        out_specs=pl.BlockSpec((tm, tn), lambda j, i: (i, j)),
        out_shape=jax.ShapeDtypeStruct((t, n), BF16),
        compiler_params=_params("parallel", "parallel"),
        name="sgu_gate_proj",
    )(h, w_zg, bias)


def _mix_kernel(u_ref, v_ref, ga_ref, gb_ref, oa_ref, x_ref, lng_ref, lnb_ref, wsp_ref, bsp_ref, wout_ref,
                g2_ref, x1_ref, h2_ref, merged_ref, *, n_groups):
    tm, width = u_ref.shape
    gd = width // n_groups
    v = v_ref[...].astype(F32)
    mu = jnp.mean(v, axis=-1, keepdims=True)
    dv = v - mu
    var = jnp.mean(dv * dv, axis=-1, keepdims=True)
    vn = ((dv * lax.rsqrt(var + EPS)) * lng_ref[...] + lnb_ref[...]).astype(BF16)
    row = lax.broadcasted_iota(jnp.int32, (CHUNK, CHUNK), 0)
    col = lax.broadcasted_iota(jnp.int32, (CHUNK, CHUNK), 1)
    causal = row >= col
    for g in range(n_groups):
        w = jnp.where(causal, wsp_ref[g], 0.0).astype(BF16)
        cs = slice(g * gd, (g + 1) * gd)
        for c in range(tm // CHUNK):
            rs = slice(c * CHUNK, (c + 1) * CHUNK)
            mixed = jnp.dot(w, vn[rs, cs], preferred_element_type=F32) + bsp_ref[:, cs]
            ob = u_ref[rs, cs].astype(F32) * mixed
            merged = ga_ref[rs, cs].astype(F32) * oa_ref[rs, cs].astype(F32) + gb_ref[rs, cs].astype(F32) * ob
            merged_ref[rs, cs] = merged.astype(merged_ref.dtype)
    y = x_ref[...] + jnp.dot(merged_ref[...], wout_ref[...], preferred_element_type=F32)
    x1_ref[...] = y
    h2_ref[...] = _rms(y, g2_ref[...]).astype(h2_ref.dtype)


def _mix_call(zg, oa, x, lng, lnb, wsp, bsp_full, wout, g2):
    t, d = x.shape
    n_groups = wsp.shape[0]
    tm = _tile(t, 256)
    row = lambda i: (i, 0)
    const = lambda i: (0, 0)
    colblk = lambda c: (lambda i: (i, c))
    return pl.pallas_call(
        functools.partial(_mix_kernel, n_groups=n_groups),
        grid=(t // tm,),
        in_specs=[
            pl.BlockSpec((tm, d), colblk(0)), pl.BlockSpec((tm, d), colblk(1)),
            pl.BlockSpec((tm, d), colblk(2)), pl.BlockSpec((tm, d), colblk(3)),
            pl.BlockSpec((tm, d), row), pl.BlockSpec((tm, d), row),
            pl.BlockSpec((1, d), const), pl.BlockSpec((1, d), const),
            pl.BlockSpec(wsp.shape, lambda i: (0, 0, 0)), pl.BlockSpec((CHUNK, d), const),
            pl.BlockSpec((d, d), const), pl.BlockSpec((1, d), const),
        ],
        out_specs=[pl.BlockSpec((tm, d), row), pl.BlockSpec((tm, d), row)],
        out_shape=[jax.ShapeDtypeStruct((t, d), F32), jax.ShapeDtypeStruct((t, d), BF16)],
        scratch_shapes=[pltpu.VMEM((tm, d), BF16)],
        compiler_params=_params("parallel"),
        name="sgu_merge_out",
    )(zg, zg, zg, zg, oa, x, lng, lnb, wsp, bsp_full, wout, g2)


def _top_values(s, k):
    vals = []
    for _ in range(k):
        m = jnp.max(s, axis=0, keepdims=True)
        vals.append(m)
        s = jnp.where(s == m, NEG_INF, s)
    return vals


def _peer_topk_kernel(h2_ref, wpq_ref, keys_ref, s1_ref, s2_ref, p1_ref, p2_ref, thr_ref, *, n_heads):
    tm = h2_ref.shape[0]
    q = jnp.dot(h2_ref[...], wpq_ref[...], preferred_element_type=F32).astype(BF16)
    for h in range(n_heads):
        st = []
        for p in range(2):
            hp = 2 * h + p
            st.append(lax.dot_general(keys_ref[hp], q[:, hp * N_KEYS:(hp + 1) * N_KEYS],
                                      (((1,), (1,)), ((), ())), preferred_element_type=F32))
        v1 = _top_values(st[0], PEER_TOPK)
        v2 = _top_values(st[1], PEER_TOPK)
        v2_all = jnp.concatenate(v2, axis=0)
        v2_top8 = v2_all[:8]
        cand = [v1[0] + v2_all] + [v1[a] + v2_top8 for a in range(1, 8)]
        cand.append(jnp.concatenate(v1[8:], axis=0) + v2[0])
        tops = _top_values(jnp.concatenate(cand, axis=0), PEER_TOPK)
        best = tops[0]
        z = jnp.ones_like(best)
        for r in range(1, PEER_TOPK):
            z = z + jnp.exp(tops[r] - best)
        p1 = jnp.exp(st[0] - v1[0])
        p2 = jnp.exp(st[1] - v2[0]) * (1.0 / z)
        for c in range(tm // LANE):
            ls = slice(c * LANE, (c + 1) * LANE)
            s1_ref[h, c] = st[0][:, ls]
            s2_ref[h, c] = st[1][:, ls]
            p1_ref[h, c] = p1[:, ls]
            p2_ref[h, c] = p2[:, ls]
            thr_ref[h, c] = tops[PEER_TOPK - 1][:, ls]


def _peer_topk_call(h2, wpq, keys):
    t, d = h2.shape
    n_heads = keys.shape[0] // 2
    tm = _tile(t, 256)
    nc = tm // LANE
    tab = pl.BlockSpec((n_heads, nc, N_KEYS, LANE), lambda i: (0, i, 0, 0))
    tab_shape = jax.ShapeDtypeStruct((n_heads, t // LANE, N_KEYS, LANE), F32)
    return pl.pallas_call(
        functools.partial(_peer_topk_kernel, n_heads=n_heads),
        grid=(t // tm,),
        in_specs=[pl.BlockSpec((tm, d), lambda i: (i, 0)), pl.BlockSpec(wpq.shape, lambda i: (0, 0)),
                  pl.BlockSpec(keys.shape, lambda i: (0, 0, 0))],
        out_specs=[tab, tab, tab, tab, pl.BlockSpec((n_heads, nc, 1, LANE), lambda i: (0, i, 0, 0))],
        out_shape=[tab_shape, tab_shape, tab_shape, tab_shape,
                   jax.ShapeDtypeStruct((n_heads, t // LANE, 1, LANE), F32)],
        compiler_params=_params("parallel"),
        name="peer_topk",
    )(h2, wpq, keys)


def _peer_dense_kernel(h2_ref, u_ref, vt_ref, s1_ref, s2_ref, p1_ref, p2_ref, thr_ref, x1_ref, gf_ref, o_ref,
                       acc_ref, a_ref, *, n_heads):
    e = pl.program_id(1)
    te = u_ref.shape[0]
    tm = h2_ref.shape[0]
    ni = te // N_KEYS

    @pl.when(e == 0)
    def _():
        acc_ref[...] = jnp.zeros_like(acc_ref)

    st = lax.dot_general(u_ref[...], h2_ref[...], (((1,), (1,)), ((), ())), preferred_element_type=F32)
    for il in range(ni):
        i_key = e * ni + il
        rs = slice(il * N_KEYS, (il + 1) * N_KEYS)
        for c in range(tm // LANE):
            ls = slice(c * LANE, (c + 1) * LANE)
            gate = jnp.zeros((N_KEYS, LANE), F32)
            for h in range(n_heads):
                s1b = s1_ref[h, c, pl.ds(i_key, 1), :]
                p1b = p1_ref[h, c, pl.ds(i_key, 1), :]
                sel = (s1b + s2_ref[h, c]) >= thr_ref[h, c]
                gate = gate + jnp.where(sel, p1b * p2_ref[h, c], 0.0)
            a_ref[rs, ls] = (_gelu_tanh(st[rs, ls]) * gate).astype(a_ref.dtype)
    acc_ref[...] += jnp.dot(vt_ref[...], a_ref[...], preferred_element_type=F32)

    @pl.when(e == pl.num_programs(1) - 1)
    def _():
        y = x1_ref[...] + acc_ref[...].T
        o_ref[...] = _rms(y, gf_ref[...]).astype(o_ref.dtype)


def _peer_dense_call(h2, u, vt, s1, s2, p1, p2, thr, x1, gf):
    t, d = h2.shape
    n_experts = u.shape[0]
    n_heads = s1.shape[0]
    tm = _tile(t, 512)
    te = _tile(n_experts, 512)
    nc = tm // LANE
    tok = lambda i, e: (i, 0)
    tab = pl.BlockSpec((n_heads, nc, N_KEYS, LANE), lambda i, e: (0, i, 0, 0))
    return pl.pallas_call(
        functools.partial(_peer_dense_kernel, n_heads=n_heads),
        grid=(t // tm, n_experts // te),
        in_specs=[
            pl.BlockSpec((tm, d), tok), pl.BlockSpec((te, d), lambda i, e: (e, 0)),
            pl.BlockSpec((d, te), lambda i, e: (0, e)),
            tab, tab, tab, tab, pl.BlockSpec((n_heads, nc, 1, LANE), lambda i, e: (0, i, 0, 0)),
            pl.BlockSpec((tm, d), tok), pl.BlockSpec((1, d), lambda i, e: (0, 0)),
        ],
        out_specs=pl.BlockSpec((tm, d), tok),
        out_shape=jax.ShapeDtypeStruct((t, d), F32),
        scratch_shapes=[pltpu.VMEM((d, tm), F32), pltpu.VMEM((te, tm), BF16)],
        compiler_params=_params("parallel", "arbitrary"),
        name="peer_dense",
    )(h2, u, vt, s1, s2, p1, p2, thr, x1, gf)


def _swap_rope_halves(w):
    half = QK_ROPE // 2
    return jnp.concatenate([w[..., half:], w[..., :half]], axis=-1)


def _layer(x2d, pos, batch, seq, g_norm1, w_in, g_q_a, w_uq, g_kv_a, w_ukv, sgu_ln_g, sgu_ln_b, w_spatial,
           b_spatial, b_gate, w_out, g_norm2, w_peer_q, peer_keys, peer_u, peer_v, g_last):
    t, d = x2d.shape
    q_lora, kv_lora = g_q_a.shape[0], g_kv_a.shape[0]
    heads = w_uq.shape[1]
    off_kpe = q_lora + kv_lora
    off_sgu = off_kpe + QK_ROPE
    row = lambda a: a.reshape(1, -1)

    w_kpe = w_in[:, off_kpe:off_sgu]
    w_lat = jnp.concatenate([w_in[:, :off_sgu], _swap_rope_halves(w_kpe)], axis=1).astype(BF16)
    w_zg = w_in[:, off_sgu:].astype(BF16)
    zg_bias = jnp.concatenate([jnp.zeros((2 * d,), F32), b_gate]).reshape(1, -1)
    wq = jnp.transpose(w_uq, (1, 0, 2))
    wq = jnp.concatenate([wq, _swap_rope_halves(wq[..., QK_NOPE:])], axis=-1).astype(BF16)
    wkv = jnp.transpose(w_ukv, (1, 0, 2)).astype(BF16)
    freqs = ROPE_THETA ** (-jnp.arange(0, QK_ROPE, 2, dtype=F32) / QK_ROPE)
    freq = jnp.concatenate([freqs, freqs]).reshape(1, -1)
    sign = jnp.concatenate([-jnp.ones((QK_ROPE // 2,), F32), jnp.ones((QK_ROPE // 2,), F32)]).reshape(1, -1)
    bsp_full = jnp.repeat(b_spatial.T, d // b_spatial.shape[0], axis=1)
    n_peer_heads = w_peer_q.shape[1]
    wpq = w_peer_q.reshape(d, -1).astype(BF16)
    keys = peer_keys.reshape(2 * n_peer_heads, N_KEYS, -1).astype(BF16)
    u = peer_u.astype(BF16)
    vt = peer_v.T.astype(BF16)

    h = _norm_call(x2d, row(g_norm1))
    cq, ckv, kpe, cos, sin = _latent_call(h, w_lat, row(g_q_a), row(g_kv_a), pos, freq, sign)
    scale = (QK_NOPE + QK_ROPE) ** -0.5 * math.log2(math.e)
    q, k, v = _head_proj_calls(cq, ckv, wq, wkv, kpe, cos, sin, batch, seq, scale)
    oa = _flash_call(q, k, v).reshape(t, heads * V_HEAD)
    zg = _zg_call(h, w_zg, zg_bias, 2 * d)
    x1, h2 = _mix_call(zg, oa, x2d, row(sgu_ln_g), row(sgu_ln_b), w_spatial, bsp_full, w_out.astype(BF16),
                       row(g_norm2))
    s1, s2, p1, p2, thr = _peer_topk_call(h2, wpq, keys)
    return _peer_dense_call(h2, u, vt, s1, s2, p1, p2, thr, x1, row(g_last))


def kernel(x, positions, g_norm1, w_in, g_q_a, w_uq, g_kv_a, w_ukv, sgu_ln_g, sgu_ln_b, w_spatial, b_spatial,
           b_gate, w_out, g_norm2, w_peer_q, peer_keys, peer_u, peer_v, g_final):
    batch, seq, d = x.shape
    depth = g_norm1.shape[0]
    assert depth == 1, "the final norm is fused into the last layer's expert kernel"
    x2d = x.reshape(batch * seq, d)
    pos = positions.astype(F32).reshape(batch * seq, 1)
    out = _layer(x2d, pos, batch, seq, g_norm1[0], w_in[0], g_q_a[0], w_uq[0], g_kv_a[0], w_ukv[0], sgu_ln_g[0],
                 sgu_ln_b[0], w_spatial[0], b_spatial[0], b_gate[0], w_out[0], g_norm2[0], w_peer_q[0],
                 peer_keys[0], peer_u[0], peer_v[0], g_final)
    return out.reshape(batch, seq, d)
```

```python
import functools
import math

import jax
import jax.numpy as jnp
from jax import lax
from jax.experimental import pallas as pl
from jax.experimental.pallas import tpu as pltpu

BF16 = jnp.bfloat16
F32 = jnp.float32

EPS = 1e-6
ROPE_THETA = 10000.0
QK_NOPE = 128
QK_ROPE = 64
V_HEAD = 128
QK_PAD = 256
CHUNK = 128
N_KEYS = 128
PEER_TOPK = 16
LANE = 128
SUBLANE = 8
KEY_TILES = N_KEYS // SUBLANE
VMEM_LIMIT_BYTES = 56 * 1024 * 1024
NEG_INF = float("-inf")
FLASH_TQ = 1024
FLASH_TK = 1024
FLASH_HEADS_PER_STEP = 2


def _params(*sem):
    return pltpu.CompilerParams(dimension_semantics=sem, vmem_limit_bytes=VMEM_LIMIT_BYTES)


def _tile(n, pref):
    t = min(n, pref)
    assert n % t == 0, (n, t)
    return t


def _gelu_tanh(x):
    c = math.sqrt(2.0 / math.pi)
    return x * (0.5 * (1.0 + jnp.tanh(c * (x + 0.044715 * (x * x * x)))))


def _rms(x, g):
    r = lax.rsqrt(jnp.mean(x * x, axis=-1, keepdims=True) + EPS)
    return (x * r) * g


def _norm_kernel(x_ref, g_ref, o_ref):
    o_ref[...] = _rms(x_ref[...], g_ref[...]).astype(o_ref.dtype)


def _norm_call(x, g):
    t, d = x.shape
    tm = _tile(t, 512)
    return pl.pallas_call(
        _norm_kernel,
        grid=(t // tm,),
        in_specs=[pl.BlockSpec((tm, d), lambda i: (i, 0)), pl.BlockSpec((1, d), lambda i: (0, 0))],
        out_specs=pl.BlockSpec((tm, d), lambda i: (i, 0)),
        out_shape=jax.ShapeDtypeStruct((t, d), BF16),
        compiler_params=_params("parallel"),
        name="norm1",
    )(x, g)


def _latent_kernel(h_ref, w_ref, gq_ref, gkv_ref, pos_ref, freq_ref, sign_ref,
                   cq_ref, ckv_ref, kpe_ref, cos_ref, sin_ref, *, q_lora, kv_lora):
    p = jnp.dot(h_ref[...], w_ref[...], preferred_element_type=F32)
    o = q_lora + kv_lora
    cq_ref[...] = _rms(p[:, :q_lora], gq_ref[...]).astype(cq_ref.dtype)
    ckv_ref[...] = _rms(p[:, q_lora:o], gkv_ref[...]).astype(ckv_ref.dtype)
    ang = pos_ref[...] * freq_ref[...]
    c = jnp.cos(ang)
    s = jnp.sin(ang) * sign_ref[...]
    cos_ref[...] = c
    sin_ref[...] = s
    kr = p[:, o:o + QK_ROPE] * c + p[:, o + QK_ROPE:o + 2 * QK_ROPE] * s
    kpe_ref[...] = jnp.concatenate([kr, jnp.zeros_like(kr)], axis=-1).astype(kpe_ref.dtype)


def _latent_call(h, w_lat, gq, gkv, pos, freq, sign):
    t, d = h.shape
    q_lora, kv_lora = gq.shape[1], gkv.shape[1]
    n = w_lat.shape[1]
    tm = _tile(t, 512)
    row = lambda i: (i, 0)
    const = lambda i: (0, 0)
    return pl.pallas_call(
        functools.partial(_latent_kernel, q_lora=q_lora, kv_lora=kv_lora),
        grid=(t // tm,),
        in_specs=[
            pl.BlockSpec((tm, d), row), pl.BlockSpec((d, n), const),
            pl.BlockSpec((1, q_lora), const), pl.BlockSpec((1, kv_lora), const),
            pl.BlockSpec((tm, 1), row), pl.BlockSpec((1, QK_ROPE), const), pl.BlockSpec((1, QK_ROPE), const),
        ],
        out_specs=[
            pl.BlockSpec((tm, q_lora), row), pl.BlockSpec((tm, kv_lora), row),
            pl.BlockSpec((tm, 2 * QK_ROPE), row), pl.BlockSpec((tm, QK_ROPE), row), pl.BlockSpec((tm, QK_ROPE), row),
        ],
        out_shape=[
            jax.ShapeDtypeStruct((t, q_lora), BF16), jax.ShapeDtypeStruct((t, kv_lora), BF16),
            jax.ShapeDtypeStruct((t, 2 * QK_ROPE), BF16),
            jax.ShapeDtypeStruct((t, QK_ROPE), F32), jax.ShapeDtypeStruct((t, QK_ROPE), F32),
        ],
        compiler_params=_params("parallel"),
        name="latent",
    )(h, w_lat, gq, gkv, pos, freq, sign)


def _q_kernel(c_ref, w_ref, cos_ref, sin_ref, q_ref, *, scale):
    r = jnp.dot(c_ref[...], w_ref[0], preferred_element_type=F32)
    qr = r[:, QK_NOPE:QK_NOPE + QK_ROPE] * cos_ref[...] + r[:, QK_NOPE + QK_ROPE:] * sin_ref[...]
    q = jnp.concatenate([r[:, :QK_NOPE], qr, jnp.zeros_like(qr)], axis=-1) * scale
    q_ref[0, 0] = q.astype(q_ref.dtype)


def _kv_kernel(c_ref, w_ref, kpe_ref, k_ref, v_ref):
    r = jnp.dot(c_ref[...], w_ref[0], preferred_element_type=F32)
    k_ref[0, 0] = jnp.concatenate([r[:, :QK_NOPE].astype(k_ref.dtype), kpe_ref[...]], axis=-1)
    v_ref[0, 0] = r[:, QK_NOPE:].astype(v_ref.dtype)


def _head_proj_calls(cq, ckv, wq, wkv, kpe, cos, sin, batch, seq, scale):
    t = cq.shape[0]
    heads = wq.shape[0]
    tm = _tile(seq, 512)
    spb = seq // tm
    row = lambda i, h: (i, 0)
    whead = lambda i, h: (h, 0, 0)
    out = lambda i, h: (i // spb, h, i % spb, 0)
    q = pl.pallas_call(
        functools.partial(_q_kernel, scale=scale),
        grid=(t // tm, heads),
        in_specs=[pl.BlockSpec((tm, cq.shape[1]), row), pl.BlockSpec((1,) + wq.shape[1:], whead),
                  pl.BlockSpec((tm, QK_ROPE), row), pl.BlockSpec((tm, QK_ROPE), row)],
        out_specs=pl.BlockSpec((1, 1, tm, QK_PAD), out),
        out_shape=jax.ShapeDtypeStruct((batch, heads, seq, QK_PAD), BF16),
        compiler_params=_params("parallel", "arbitrary"),
        name="q_proj",
    )(cq, wq, cos, sin)
    k, v = pl.pallas_call(
        _kv_kernel,
        grid=(t // tm, heads),
        in_specs=[pl.BlockSpec((tm, ckv.shape[1]), row), pl.BlockSpec((1,) + wkv.shape[1:], whead),
                  pl.BlockSpec((tm, 2 * QK_ROPE), row)],
        out_specs=[pl.BlockSpec((1, 1, tm, QK_PAD), out), pl.BlockSpec((1, 1, tm, V_HEAD), out)],
        out_shape=[jax.ShapeDtypeStruct((batch, heads, seq, QK_PAD), BF16),
                   jax.ShapeDtypeStruct((batch, heads, seq, V_HEAD), BF16)],
        compiler_params=_params("parallel", "arbitrary"),
        name="kv_proj",
    )(ckv, wkv, kpe)
    return q, k, v


def _flash_kernel(qi_tab, ki_tab, q_ref, k_ref, v_ref, o_ref, m_ref, l_ref, acc_ref, *, tq, tk, group):
    t = pl.program_id(2)
    qi = qi_tab[t]
    ki = ki_tab[t]

    @pl.when(ki == 0)
    def _():
        m_ref[...] = jnp.full_like(m_ref, NEG_INF)
        l_ref[...] = jnp.zeros_like(l_ref)
        acc_ref[...] = jnp.zeros_like(acc_ref)

    def update(masked):
        for g in range(group):
            s = lax.dot_general(q_ref[0, g], k_ref[0, g], (((1,), (1,)), ((), ())),
                                preferred_element_type=F32)
            if masked:
                row = qi * tq + lax.broadcasted_iota(jnp.int32, s.shape, 0)
                col = ki * tk + lax.broadcasted_iota(jnp.int32, s.shape, 1)
                s = jnp.where(row >= col, s, NEG_INF)
            m_prev = m_ref[g]
            m_new = jnp.maximum(m_prev, jnp.max(s, axis=-1, keepdims=True))
            alpha = jnp.exp2(m_prev - m_new)
            ps = [jnp.exp2(s[:, j * LANE:(j + 1) * LANE] - m_new) for j in range(tk // LANE)]
            l_ref[g] = alpha * l_ref[g] + functools.reduce(lambda a, b: a + b, ps)
            p = jnp.concatenate(ps, axis=-1).astype(v_ref.dtype)
            acc_ref[g] = alpha * acc_ref[g] + jnp.dot(p, v_ref[0, g], preferred_element_type=F32)
            m_ref[g] = m_new

    crosses_diagonal = (ki + 1) * tk - 1 > qi * tq

    @pl.when(jnp.logical_not(crosses_diagonal))
    def _():
        update(False)

    @pl.when(crosses_diagonal)
    def _():
        update(True)

    @pl.when(ki == ((qi + 1) * tq - 1) // tk)
    def _():
        for g in range(group):
            l = jnp.sum(l_ref[g], axis=-1, keepdims=True)
            o_ref[0, :, g * V_HEAD:(g + 1) * V_HEAD] = (acc_ref[g] / l).astype(o_ref.dtype)


def _flash_call(q, k, v):
    batch, heads, seq, _ = q.shape
    tq = _tile(seq, FLASH_TQ)
    tk = _tile(seq, FLASH_TK)
    group = _tile(heads, FLASH_HEADS_PER_STEP)
    pairs = [(i, j) for i in range(seq // tq) for j in range(((i + 1) * tq - 1) // tk + 1)]
    qi_tab = jnp.asarray([p[0] for p in pairs], jnp.int32)
    ki_tab = jnp.asarray([p[1] for p in pairs], jnp.int32)
    qmap = lambda b, h, t, qt, kt: (b, h, qt[t], 0)
    kmap = lambda b, h, t, qt, kt: (b, h, kt[t], 0)
    return pl.pallas_call(
        functools.partial(_flash_kernel, tq=tq, tk=tk, group=group),
        grid_spec=pltpu.PrefetchScalarGridSpec(
            num_scalar_prefetch=2,
            grid=(batch, heads // group, len(pairs)),
            in_specs=[pl.BlockSpec((1, group, tq, QK_PAD), qmap), pl.BlockSpec((1, group, tk, QK_PAD), kmap),
                      pl.BlockSpec((1, group, tk, V_HEAD), kmap)],
            out_specs=pl.BlockSpec((1, tq, group * V_HEAD), lambda b, h, t, qt, kt: (b, qt[t], h)),
            scratch_shapes=[pltpu.VMEM((group, tq, LANE), F32), pltpu.VMEM((group, tq, LANE), F32),
                            pltpu.VMEM((group, tq, V_HEAD), F32)],
        ),
        out_shape=jax.ShapeDtypeStruct((batch, seq, heads * V_HEAD), BF16),
        compiler_params=_params("parallel", "parallel", "arbitrary"),
        name="flash_attn",
    )(qi_tab, ki_tab, q, k, v)


def _zg_kernel(h_ref, w_ref, b_ref, o_ref, *, n_gelu_blocks):
    j = pl.program_id(0)
    z = jnp.dot(h_ref[...], w_ref[...], preferred_element_type=F32)

    @pl.when(j < n_gelu_blocks)
    def _():
        o_ref[...] = _gelu_tanh(z).astype(o_ref.dtype)

    @pl.when(j >= n_gelu_blocks)
    def _():
        o_ref[...] = (1.0 / (1.0 + jnp.exp(-(z + b_ref[...])))).astype(o_ref.dtype)


def _zg_call(h, w_zg, bias, n_gelu_cols):
    t, d = h.shape
    n = w_zg.shape[1]
    tm = _tile(t, 512)
    tn = _tile(n_gelu_cols, 1024)
    return pl.pallas_call(
        functools.partial(_zg_kernel, n_gelu_blocks=n_gelu_cols // tn),
        grid=(n // tn, t // tm),
        in_specs=[pl.BlockSpec((tm, d), lambda j, i: (i, 0)), pl.BlockSpec((d, tn), lambda j, i: (0, j)),
                  pl.BlockSpec((1, tn), lambda j, i: (0, j))],
        out_specs=pl.BlockSpec((tm, tn), lambda j, i: (i, j)),
        out_shape=jax.ShapeDtypeStruct((t, n), BF16),
        compiler_params=_params("parallel", "parallel"),
        name="sgu_gate_proj",
    )(h, w_zg, bias)


def _mix_kernel(u_ref, v_ref, ga_ref, gb_ref, oa_ref, x_ref, lng_ref, lnb_ref, wsp_ref, bsp_ref, wout_ref,
                g2_ref, x1_ref, h2_ref, merged_ref, *, n_groups):
    tm, width = u_ref.shape
    gd = width // n_groups
    v = v_ref[...].astype(F32)
    mu = jnp.mean(v, axis=-1, keepdims=True)
    dv = v - mu
    var = jnp.mean(dv * dv, axis=-1, keepdims=True)
    vn = ((dv * lax.rsqrt(var + EPS)) * lng_ref[...] + lnb_ref[...]).astype(BF16)
    row = lax.broadcasted_iota(jnp.int32, (CHUNK, CHUNK), 0)
    col = lax.broadcasted_iota(jnp.int32, (CHUNK, CHUNK), 1)
    causal = row >= col
    for g in range(n_groups):
        w = jnp.where(causal, wsp_ref[g], 0.0).astype(BF16)
        cs = slice(g * gd, (g + 1) * gd)
        for c in range(tm // CHUNK):
            rs = slice(c * CHUNK, (c + 1) * CHUNK)
            mixed = jnp.dot(w, vn[rs, cs], preferred_element_type=F32) + bsp_ref[:, cs]
            ob = u_ref[rs, cs].astype(F32) * mixed
            merged = ga_ref[rs, cs].astype(F32) * oa_ref[rs, cs].astype(F32) + gb_ref[rs, cs].astype(F32) * ob
            merged_ref[rs, cs] = merged.astype(merged_ref.dtype)
    y = x_ref[...] + jnp.dot(merged_ref[...], wout_ref[...], preferred_element_type=F32)
    x1_ref[...] = y
    h2_ref[...] = _rms(y, g2_ref[...]).astype(h2_ref.dtype)


def _mix_call(zg, oa, x, lng, lnb, wsp, bsp_full, wout, g2):
    t, d = x.shape
    n_groups = wsp.shape[0]
    tm = _tile(t, 256)
    row = lambda i: (i, 0)
    const = lambda i: (0, 0)
    colblk = lambda c: (lambda i: (i, c))
    return pl.pallas_call(
        functools.partial(_mix_kernel, n_groups=n_groups),
        grid=(t // tm,),
        in_specs=[
            pl.BlockSpec((tm, d), colblk(0)), pl.BlockSpec((tm, d), colblk(1)),
            pl.BlockSpec((tm, d), colblk(2)), pl.BlockSpec((tm, d), colblk(3)),
            pl.BlockSpec((tm, d), row), pl.BlockSpec((tm, d), row),
            pl.BlockSpec((1, d), const), pl.BlockSpec((1, d), const),
            pl.BlockSpec(wsp.shape, lambda i: (0, 0, 0)), pl.BlockSpec((CHUNK, d), const),
            pl.BlockSpec((d, d), const), pl.BlockSpec((1, d), const),
        ],
        out_specs=[pl.BlockSpec((tm, d), row), pl.BlockSpec((tm, d), row)],
        out_shape=[jax.ShapeDtypeStruct((t, d), F32), jax.ShapeDtypeStruct((t, d), BF16)],
        scratch_shapes=[pltpu.VMEM((tm, d), BF16)],
        compiler_params=_params("parallel"),
        name="sgu_merge_out",
    )(zg, zg, zg, zg, oa, x, lng, lnb, wsp, bsp_full, wout, g2)


def _top_values(s, k):
    vals = []
    for _ in range(k):
        m = jnp.max(s, axis=0, keepdims=True)
        vals.append(m)
        s = jnp.where(s == m, NEG_INF, s)
    return vals


def _peer_topk_kernel(h2_ref, wpq_ref, keys_ref, s1_ref, s2_ref, p1_ref, p2_ref, thr_ref, *, n_heads):
    tm = h2_ref.shape[0]
    q = jnp.dot(h2_ref[...], wpq_ref[...], preferred_element_type=F32).astype(BF16)
    for h in range(n_heads):
        st = []
        for p in range(2):
            hp = 2 * h + p
            st.append(lax.dot_general(keys_ref[hp], q[:, hp * N_KEYS:(hp + 1) * N_KEYS],
                                      (((1,), (1,)), ((), ())), preferred_element_type=F32))
        v1 = _top_values(st[0], PEER_TOPK)
        v2 = _top_values(st[1], PEER_TOPK)
        v2_all = jnp.concatenate(v2, axis=0)
        v2_top8 = v2_all[:8]
        cand = [v1[0] + v2_all] + [v1[a] + v2_top8 for a in range(1, 8)]
        cand.append(jnp.concatenate(v1[8:], axis=0) + v2[0])
        tops = _top_values(jnp.concatenate(cand, axis=0), PEER_TOPK)
        best = tops[0]
        z = jnp.ones_like(best)
        for r in range(1, PEER_TOPK):
            z = z + jnp.exp(tops[r] - best)
        p1 = jnp.exp(st[0] - v1[0])
        p2 = jnp.exp(st[1] - v2[0]) * (1.0 / z)
        for c in range(tm // LANE):
            ls = slice(c * LANE, (c + 1) * LANE)
            tile = lambda a: a[:, ls].reshape(KEY_TILES, SUBLANE, LANE)
            s1_ref[h, c] = tile(st[0])
            s2_ref[h, c] = tile(st[1])
            p1_ref[h, c] = tile(p1)
            p2_ref[h, c] = tile(p2)
            thr_ref[h, c] = tops[PEER_TOPK - 1][:, ls]


def _peer_topk_call(h2, wpq, keys):
    t, d = h2.shape
    n_heads = keys.shape[0] // 2
    tm = _tile(t, 256)
    nc = tm // LANE
    tab = pl.BlockSpec((n_heads, nc, KEY_TILES, SUBLANE, LANE), lambda i: (0, i, 0, 0, 0))
    tab_shape = jax.ShapeDtypeStruct((n_heads, t // LANE, KEY_TILES, SUBLANE, LANE), F32)
    return pl.pallas_call(
        functools.partial(_peer_topk_kernel, n_heads=n_heads),
        grid=(t // tm,),
        in_specs=[pl.BlockSpec((tm, d), lambda i: (i, 0)), pl.BlockSpec(wpq.shape, lambda i: (0, 0)),
                  pl.BlockSpec(keys.shape, lambda i: (0, 0, 0))],
        out_specs=[tab, tab, tab, tab, pl.BlockSpec((n_heads, nc, 1, LANE), lambda i: (0, i, 0, 0))],
        out_shape=[tab_shape, tab_shape, tab_shape, tab_shape,
                   jax.ShapeDtypeStruct((n_heads, t // LANE, 1, LANE), F32)],
        compiler_params=_params("parallel"),
        name="peer_topk",
    )(h2, wpq, keys)


def _peer_dense_kernel(h2_ref, u_ref, vt_ref, s1_ref, s2_ref, p1_ref, p2_ref, thr_ref, x1_ref, gf_ref, o_ref,
                       acc_ref, st0_ref, st1_ref, a0_ref, a1_ref, *, n_heads, n_pairs):
    g = pl.program_id(1)
    te = st0_ref.shape[0]
    tm = h2_ref.shape[0]
    ni = te // N_KEYS

    @pl.when(g == 0)
    def _():
        acc_ref[...] = jnp.zeros_like(acc_ref)
        for r in (st0_ref, st1_ref, a0_ref, a1_ref):
            r[...] = jnp.zeros_like(r)

    def stages(half, st_new, st_old, a_new, a_old, act_block, act_valid):
        rows = slice(half * te, (half + 1) * te)
        acc_ref[...] += jnp.dot(vt_ref[:, rows], a_old[...], preferred_element_type=F32)
        blk = jnp.clip(act_block, 0, 2 * n_pairs - 1)
        for c in range(tm // LANE):
            ls = slice(c * LANE, (c + 1) * LANE)
            thr = [jnp.broadcast_to(jnp.where(act_valid, thr_ref[h, c], jnp.inf), (SUBLANE, LANE))
                   for h in range(n_heads)]
            for il in range(ni):
                i_key = blk * ni + il
                hi = lax.shift_right_logical(i_key, 3)
                lo = lax.bitwise_and(i_key, SUBLANE - 1)
                rs = slice(il * N_KEYS, (il + 1) * N_KEYS)
                gate = jnp.zeros((KEY_TILES, SUBLANE, LANE), F32)
                for h in range(n_heads):
                    s1b = jnp.broadcast_to(s1_ref[h, c, hi, pl.ds(lo, 1), :], (SUBLANE, LANE))
                    p1b = jnp.broadcast_to(p1_ref[h, c, hi, pl.ds(lo, 1), :], (SUBLANE, LANE))
                    sel = (s1b[None] + s2_ref[h, c]) >= thr[h][None]
                    gate = gate + jnp.where(sel, p1b[None] * p2_ref[h, c], 0.0)
                s = st_old[rs, ls].reshape(KEY_TILES, SUBLANE, LANE)
                a_new[rs, ls] = (_gelu_tanh(s) * gate).reshape(N_KEYS, LANE).astype(a_new.dtype)
        st_new[...] = lax.dot_general(u_ref[rows, :], h2_ref[...], (((1,), (1,)), ((), ())),
                                      preferred_element_type=F32)

    stages(0, st0_ref, st1_ref, a1_ref, a0_ref, 2 * g - 1, g >= 1)
    stages(1, st1_ref, st0_ref, a0_ref, a1_ref, 2 * g, g < n_pairs)

    @pl.when(g == n_pairs)
    def _():
        y = x1_ref[...] + acc_ref[...].T
        o_ref[...] = _rms(y, gf_ref[...]).astype(o_ref.dtype)


def _peer_dense_call(h2, u, vt, s1, s2, p1, p2, thr, x1, gf):
    t, d = h2.shape
    n_experts = u.shape[0]
    n_heads = s1.shape[0]
    tm = _tile(t, 512)
    te = _tile(n_experts, 512)
    nc = tm // LANE
    n_pairs = n_experts // (2 * te)
    tok = lambda i, g: (i, 0)
    once = pl.Buffered(1)
    tab = pl.BlockSpec((n_heads, nc, KEY_TILES, SUBLANE, LANE), lambda i, g: (0, i, 0, 0, 0), pipeline_mode=once)
    return pl.pallas_call(
        functools.partial(_peer_dense_kernel, n_heads=n_heads, n_pairs=n_pairs),
        grid=(t // tm, n_pairs + 1),
        in_specs=[
            pl.BlockSpec((tm, d), tok),
            pl.BlockSpec((2 * te, d), lambda i, g: (jnp.minimum(g, n_pairs - 1), 0)),
            pl.BlockSpec((d, 2 * te), lambda i, g: (0, jnp.maximum(g - 1, 0))),
            tab, tab, tab, tab, pl.BlockSpec((n_heads, nc, 1, LANE), lambda i, g: (0, i, 0, 0)),
            pl.BlockSpec((tm, d), tok, pipeline_mode=once), pl.BlockSpec((1, d), lambda i, g: (0, 0)),
        ],
        out_specs=pl.BlockSpec((tm, d), tok),
        out_shape=jax.ShapeDtypeStruct((t, d), F32),
        scratch_shapes=[pltpu.VMEM((d, tm), F32), pltpu.VMEM((te, tm), F32), pltpu.VMEM((te, tm), F32),
                        pltpu.VMEM((te, tm), BF16), pltpu.VMEM((te, tm), BF16)],
        compiler_params=_params("parallel", "arbitrary"),
        name="peer_dense",
    )(h2, u, vt, s1, s2, p1, p2, thr, x1, gf)


def _swap_rope_halves(w):
    half = QK_ROPE // 2
    return jnp.concatenate([w[..., half:], w[..., :half]], axis=-1)


def _layer(x2d, pos, batch, seq, g_norm1, w_in, g_q_a, w_uq, g_kv_a, w_ukv, sgu_ln_g, sgu_ln_b, w_spatial,
           b_spatial, b_gate, w_out, g_norm2, w_peer_q, peer_keys, peer_u, peer_v, g_last):
    t, d = x2d.shape
    q_lora, kv_lora = g_q_a.shape[0], g_kv_a.shape[0]
    heads = w_uq.shape[1]
    off_kpe = q_lora + kv_lora
    off_sgu = off_kpe + QK_ROPE
    row = lambda a: a.reshape(1, -1)

    w_kpe = w_in[:, off_kpe:off_sgu]
    w_lat = jnp.concatenate([w_in[:, :off_sgu], _swap_rope_halves(w_kpe)], axis=1).astype(BF16)
    w_zg = w_in[:, off_sgu:].astype(BF16)
    zg_bias = jnp.concatenate([jnp.zeros((2 * d,), F32), b_gate]).reshape(1, -1)
    wq = jnp.transpose(w_uq, (1, 0, 2))
    wq = jnp.concatenate([wq, _swap_rope_halves(wq[..., QK_NOPE:])], axis=-1).astype(BF16)
    wkv = jnp.transpose(w_ukv, (1, 0, 2)).astype(BF16)
    freqs = ROPE_THETA ** (-jnp.arange(0, QK_ROPE, 2, dtype=F32) / QK_ROPE)
    freq = jnp.concatenate([freqs, freqs]).reshape(1, -1)
    sign = jnp.concatenate([-jnp.ones((QK_ROPE // 2,), F32), jnp.ones((QK_ROPE // 2,), F32)]).reshape(1, -1)
    bsp_full = jnp.repeat(b_spatial.T, d // b_spatial.shape[0], axis=1)
    n_peer_heads = w_peer_q.shape[1]
    wpq = w_peer_q.reshape(d, -1).astype(BF16)
    keys = peer_keys.reshape(2 * n_peer_heads, N_KEYS, -1).astype(BF16)
    u = peer_u.astype(BF16)
    vt = peer_v.T.astype(BF16)

    h = _norm_call(x2d, row(g_norm1))
    cq, ckv, kpe, cos, sin = _latent_call(h, w_lat, row(g_q_a), row(g_kv_a), pos, freq, sign)
    scale = (QK_NOPE + QK_ROPE) ** -0.5 * math.log2(math.e)
    q, k, v = _head_proj_calls(cq, ckv, wq, wkv, kpe, cos, sin, batch, seq, scale)
    oa = _flash_call(q, k, v).reshape(t, heads * V_HEAD)
    zg = _zg_call(h, w_zg, zg_bias, 2 * d)
    x1, h2 = _mix_call(zg, oa, x2d, row(sgu_ln_g), row(sgu_ln_b), w_spatial, bsp_full, w_out.astype(BF16),
                       row(g_norm2))
    s1, s2, p1, p2, thr = _peer_topk_call(h2, wpq, keys)
    return _peer_dense_call(h2, u, vt, s1, s2, p1, p2, thr, x1, row(g_last))


def kernel(x, positions, g_norm1, w_in, g_q_a, w_uq, g_kv_a, w_ukv, sgu_ln_g, sgu_ln_b, w_spatial, b_spatial,
           b_gate, w_out, g_norm2, w_peer_q, peer_keys, peer_u, peer_v, g_final):
    batch, seq, d = x.shape
    depth = g_norm1.shape[0]
    assert depth == 1, "the final norm is fused into the last layer's expert kernel"
    x2d = x.reshape(batch * seq, d)
    pos = positions.astype(F32).reshape(batch * seq, 1)
    out = _layer(x2d, pos, batch, seq, g_norm1[0], w_in[0], g_q_a[0], w_uq[0], g_kv_a[0], w_ukv[0], sgu_ln_g[0],
                 sgu_ln_b[0], w_spatial[0], b_spatial[0], b_gate[0], w_out[0], g_norm2[0], w_peer_q[0],
                 peer_keys[0], peer_u[0], peer_v[0], g_final)
    return out.reshape(batch, seq, d)
```

```python
import functools
import math

import jax
import jax.numpy as jnp
from jax import lax
from jax.experimental import pallas as pl
from jax.experimental.pallas import tpu as pltpu

BF16 = jnp.bfloat16
F32 = jnp.float32

EPS = 1e-6
ROPE_THETA = 10000.0
QK_NOPE = 128
QK_ROPE = 64
V_HEAD = 128
QK_PAD = 256
CHUNK = 128
N_KEYS = 128
PEER_TOPK = 16
LANE = 128
SUBLANE = 8
KEY_TILES = N_KEYS // SUBLANE
MXU_WIDTH = 256
VMEM_LIMIT_BYTES = 56 * 1024 * 1024
NEG_INF = float("-inf")
FLASH_TQ = 1024
FLASH_TK = 1024
FLASH_HEADS_PER_STEP = 2


def _params(*sem, flags=None):
    return pltpu.CompilerParams(dimension_semantics=sem, vmem_limit_bytes=VMEM_LIMIT_BYTES, flags=flags)


def _tile(n, pref):
    t = min(n, pref)
    assert n % t == 0, (n, t)
    return t


def _gelu_tanh(x):
    c = math.sqrt(2.0 / math.pi)
    return x * (0.5 * (1.0 + jnp.tanh(c * (x + 0.044715 * (x * x * x)))))


def _rms(x, g):
    r = lax.rsqrt(jnp.mean(x * x, axis=-1, keepdims=True) + EPS)
    return (x * r) * g


def _norm_kernel(x_ref, g_ref, o_ref):
    o_ref[...] = _rms(x_ref[...], g_ref[...]).astype(o_ref.dtype)


def _norm_call(x, g):
    t, d = x.shape
    tm = _tile(t, 512)
    return pl.pallas_call(
        _norm_kernel,
        grid=(t // tm,),
        in_specs=[pl.BlockSpec((tm, d), lambda i: (i, 0)), pl.BlockSpec((1, d), lambda i: (0, 0))],
        out_specs=pl.BlockSpec((tm, d), lambda i: (i, 0)),
        out_shape=jax.ShapeDtypeStruct((t, d), BF16),
        compiler_params=_params("parallel"),
        name="norm1",
    )(x, g)


def _latent_kernel(h_ref, w_ref, gq_ref, gkv_ref, pos_ref, freq_ref, sign_ref,
                   cq_ref, ckv_ref, kpe_ref, cos_ref, sin_ref, *, q_lora, kv_lora):
    p = jnp.dot(h_ref[...], w_ref[...], preferred_element_type=F32)
    o = q_lora + kv_lora
    cq_ref[...] = _rms(p[:, :q_lora], gq_ref[...]).astype(cq_ref.dtype)
    ckv_ref[...] = _rms(p[:, q_lora:o], gkv_ref[...]).astype(ckv_ref.dtype)
    ang = pos_ref[...] * freq_ref[...]
    c = jnp.cos(ang)
    s = jnp.sin(ang) * sign_ref[...]
    cos_ref[...] = c
    sin_ref[...] = s
    kr = p[:, o:o + QK_ROPE] * c + p[:, o + QK_ROPE:o + 2 * QK_ROPE] * s
    kpe_ref[...] = jnp.concatenate([kr, jnp.zeros_like(kr)], axis=-1).astype(kpe_ref.dtype)


def _latent_call(h, w_lat, gq, gkv, pos, freq, sign):
    t, d = h.shape
    q_lora, kv_lora = gq.shape[1], gkv.shape[1]
    n = w_lat.shape[1]
    tm = _tile(t, 512)
    row = lambda i: (i, 0)
    const = lambda i: (0, 0)
    return pl.pallas_call(
        functools.partial(_latent_kernel, q_lora=q_lora, kv_lora=kv_lora),
        grid=(t // tm,),
        in_specs=[
            pl.BlockSpec((tm, d), row), pl.BlockSpec((d, n), const),
            pl.BlockSpec((1, q_lora), const), pl.BlockSpec((1, kv_lora), const),
            pl.BlockSpec((tm, 1), row), pl.BlockSpec((1, QK_ROPE), const), pl.BlockSpec((1, QK_ROPE), const),
        ],
        out_specs=[
            pl.BlockSpec((tm, q_lora), row), pl.BlockSpec((tm, kv_lora), row),
            pl.BlockSpec((tm, 2 * QK_ROPE), row), pl.BlockSpec((tm, QK_ROPE), row), pl.BlockSpec((tm, QK_ROPE), row),
        ],
        out_shape=[
            jax.ShapeDtypeStruct((t, q_lora), BF16), jax.ShapeDtypeStruct((t, kv_lora), BF16),
            jax.ShapeDtypeStruct((t, 2 * QK_ROPE), BF16),
            jax.ShapeDtypeStruct((t, QK_ROPE), F32), jax.ShapeDtypeStruct((t, QK_ROPE), F32),
        ],
        compiler_params=_params("parallel"),
        name="latent",
    )(h, w_lat, gq, gkv, pos, freq, sign)


def _head_proj_kernel(cq_ref, ckv_ref, wq_ref, wkv_ref, kpe_ref, cos_ref, sin_ref, q_ref, k_ref, v_ref, *, scale):
    cq = cq_ref[...]
    ckv = ckv_ref[...]
    kpe = kpe_ref[...]
    cos = cos_ref[...]
    sin = sin_ref[...]
    for g in range(wq_ref.shape[0]):
        r = jnp.dot(cq, wq_ref[g], preferred_element_type=F32)
        qr = r[:, QK_NOPE:QK_NOPE + QK_ROPE] * cos + r[:, QK_NOPE + QK_ROPE:] * sin
        q = jnp.concatenate([r[:, :QK_NOPE], qr, jnp.zeros_like(qr)], axis=-1) * scale
        q_ref[0, g] = q.astype(q_ref.dtype)
        r = jnp.dot(ckv, wkv_ref[g], preferred_element_type=F32)
        k_ref[0, g] = jnp.concatenate([r[:, :QK_NOPE].astype(k_ref.dtype), kpe], axis=-1)
        v_ref[0, g] = r[:, QK_NOPE:].astype(v_ref.dtype)


def _head_proj_calls(cq, ckv, wq, wkv, kpe, cos, sin, batch, seq, scale):
    t = cq.shape[0]
    heads = wq.shape[0]
    tm = _tile(seq, 512)
    group = _tile(heads, 8)
    spb = seq // tm
    row = lambda i, h: (i, 0)
    whead = lambda i, h: (h, 0, 0)
    out = lambda i, h: (i // spb, h, i % spb, 0)
    return pl.pallas_call(
        functools.partial(_head_proj_kernel, scale=scale),
        grid=(t // tm, heads // group),
        in_specs=[pl.BlockSpec((tm, cq.shape[1]), row), pl.BlockSpec((tm, ckv.shape[1]), row),
                  pl.BlockSpec((group,) + wq.shape[1:], whead), pl.BlockSpec((group,) + wkv.shape[1:], whead),
                  pl.BlockSpec((tm, 2 * QK_ROPE), row), pl.BlockSpec((tm, QK_ROPE), row),
                  pl.BlockSpec((tm, QK_ROPE), row)],
        out_specs=[pl.BlockSpec((1, group, tm, QK_PAD), out), pl.BlockSpec((1, group, tm, QK_PAD), out),
                   pl.BlockSpec((1, group, tm, V_HEAD), out)],
        out_shape=[jax.ShapeDtypeStruct((batch, heads, seq, QK_PAD), BF16),
                   jax.ShapeDtypeStruct((batch, heads, seq, QK_PAD), BF16),
                   jax.ShapeDtypeStruct((batch, heads, seq, V_HEAD), BF16)],
        compiler_params=_params("parallel", "arbitrary"),
        name="head_proj",
    )(cq, ckv, wq, wkv, kpe, cos, sin)


def _flash_kernel(qi_tab, ki_tab, q_ref, k_ref, v_ref, o_ref, m_ref, l_ref, acc_ref, *, tq, tk, group):
    t = pl.program_id(2)
    qi = qi_tab[t]
    ki = ki_tab[t]

    @pl.when(ki == 0)
    def _():
        m_ref[...] = jnp.full_like(m_ref, NEG_INF)
        l_ref[...] = jnp.zeros_like(l_ref)
        acc_ref[...] = jnp.zeros_like(acc_ref)

    def update(masked):
        for g in range(group):
            s = lax.dot_general(q_ref[0, g], k_ref[0, g], (((1,), (1,)), ((), ())),
                                preferred_element_type=F32)
            if masked:
                row = qi * tq + lax.broadcasted_iota(jnp.int32, s.shape, 0)
                col = ki * tk + lax.broadcasted_iota(jnp.int32, s.shape, 1)
                s = jnp.where(row >= col, s, NEG_INF)
            m_prev = m_ref[g]
            m_new = jnp.maximum(m_prev, jnp.max(s, axis=-1, keepdims=True))
            alpha = jnp.exp2(m_prev - m_new)
            ps = [jnp.exp2(s[:, j * LANE:(j + 1) * LANE] - m_new) for j in range(tk // LANE)]
            l_ref[g] = alpha * l_ref[g] + functools.reduce(lambda a, b: a + b, ps)
            p = jnp.concatenate(ps, axis=-1).astype(v_ref.dtype)
            acc_ref[g] = alpha * acc_ref[g] + jnp.dot(p, v_ref[0, g], preferred_element_type=F32)
            m_ref[g] = m_new

    crosses_diagonal = (ki + 1) * tk - 1 > qi * tq

    @pl.when(jnp.logical_not(crosses_diagonal))
    def _():
        update(False)

    @pl.when(crosses_diagonal)
    def _():
        update(True)

    @pl.when(ki == ((qi + 1) * tq - 1) // tk)
    def _():
        for g in range(group):
            l = jnp.sum(l_ref[g], axis=-1, keepdims=True)
            o_ref[0, :, g * V_HEAD:(g + 1) * V_HEAD] = (acc_ref[g] / l).astype(o_ref.dtype)


def _flash_call(q, k, v):
    batch, heads, seq, _ = q.shape
    tq = _tile(seq, FLASH_TQ)
    tk = _tile(seq, FLASH_TK)
    group = _tile(heads, FLASH_HEADS_PER_STEP)
    pairs = [(i, j) for i in range(seq // tq) for j in range(((i + 1) * tq - 1) // tk + 1)]
    qi_tab = jnp.asarray([p[0] for p in pairs], jnp.int32)
    ki_tab = jnp.asarray([p[1] for p in pairs], jnp.int32)
    qmap = lambda b, h, t, qt, kt: (b, h, qt[t], 0)
    kmap = lambda b, h, t, qt, kt: (b, h, kt[t], 0)
    return pl.pallas_call(
        functools.partial(_flash_kernel, tq=tq, tk=tk, group=group),
        grid_spec=pltpu.PrefetchScalarGridSpec(
            num_scalar_prefetch=2,
            grid=(batch, heads // group, len(pairs)),
            in_specs=[pl.BlockSpec((1, group, tq, QK_PAD), qmap), pl.BlockSpec((1, group, tk, QK_PAD), kmap),
                      pl.BlockSpec((1, group, tk, V_HEAD), kmap)],
            out_specs=pl.BlockSpec((1, tq, group * V_HEAD), lambda b, h, t, qt, kt: (b, qt[t], h)),
            scratch_shapes=[pltpu.VMEM((group, tq, LANE), F32), pltpu.VMEM((group, tq, LANE), F32),
                            pltpu.VMEM((group, tq, V_HEAD), F32)],
        ),
        out_shape=jax.ShapeDtypeStruct((batch, seq, heads * V_HEAD), BF16),
        compiler_params=_params("parallel", "parallel", "arbitrary"),
        name="flash_attn",
    )(qi_tab, ki_tab, q, k, v)


def _zg_kernel(h_ref, w_ref, b_ref, o_ref, *, n_gelu_blocks):
    j = pl.program_id(0)
    z = jnp.dot(h_ref[...], w_ref[...], preferred_element_type=F32)

    @pl.when(j < n_gelu_blocks)
    def _():
        o_ref[...] = _gelu_tanh(z).astype(o_ref.dtype)

    @pl.when(j >= n_gelu_blocks)
    def _():
        o_ref[...] = (1.0 / (1.0 + jnp.exp(-(z + b_ref[...])))).astype(o_ref.dtype)


def _zg_call(h, w_zg, bias, n_gelu_cols):
    t, d = h.shape
    n = w_zg.shape[1]
    tm = _tile(t, 1024)
    tn = _tile(n_gelu_cols, 1024)
    return pl.pallas_call(
        functools.partial(_zg_kernel, n_gelu_blocks=n_gelu_cols // tn),
        grid=(n // tn, t // tm),
        in_specs=[pl.BlockSpec((tm, d), lambda j, i: (i, 0)), pl.BlockSpec((d, tn), lambda j, i: (0, j)),
                  pl.BlockSpec((1, tn), lambda j, i: (0, j))],
        out_specs=pl.BlockSpec((tm, tn), lambda j, i: (i, j)),
        out_shape=jax.ShapeDtypeStruct((t, n), BF16),
        compiler_params=_params("parallel", "parallel"),
        name="sgu_gate_proj",
    )(h, w_zg, bias)


def _mix_kernel(u_ref, v_ref, ga_ref, gb_ref, oa_ref, x_ref, lng_ref, lnb_ref, wsp_ref, bsp_ref, wout_ref,
                g2_ref, x1_ref, h2_ref, merged_ref, *, n_groups):
    tm, width = u_ref.shape
    gd = width // n_groups
    v = v_ref[...].astype(F32)
    mu = jnp.mean(v, axis=-1, keepdims=True)
    dv = v - mu
    var = jnp.mean(dv * dv, axis=-1, keepdims=True)
    vn = ((dv * lax.rsqrt(var + EPS)) * lng_ref[...] + lnb_ref[...]).astype(BF16)
    row = lax.broadcasted_iota(jnp.int32, (CHUNK, CHUNK), 0)
    col = lax.broadcasted_iota(jnp.int32, (CHUNK, CHUNK), 1)
    causal = row >= col
    for g in range(n_groups):
        w = jnp.where(causal, wsp_ref[g], 0.0).astype(BF16)
        cs = slice(g * gd, (g + 1) * gd)
        for c in range(tm // CHUNK):
            rs = slice(c * CHUNK, (c + 1) * CHUNK)
            mixed = jnp.dot(w, vn[rs, cs], preferred_element_type=F32) + bsp_ref[:, cs]
            ob = u_ref[rs, cs].astype(F32) * mixed
            merged = ga_ref[rs, cs].astype(F32) * oa_ref[rs, cs].astype(F32) + gb_ref[rs, cs].astype(F32) * ob
            merged_ref[rs, cs] = merged.astype(merged_ref.dtype)
    y = x_ref[...] + jnp.dot(merged_ref[...], wout_ref[...], preferred_element_type=F32)
    x1_ref[...] = y
    h2_ref[...] = _rms(y, g2_ref[...]).astype(h2_ref.dtype)


def _mix_call(zg, oa, x, lng, lnb, wsp, bsp_full, wout, g2):
    t, d = x.shape
    n_groups = wsp.shape[0]
    tm = _tile(t, 256)
    row = lambda i: (i, 0)
    const = lambda i: (0, 0)
    colblk = lambda c: (lambda i: (i, c))
    return pl.pallas_call(
        functools.partial(_mix_kernel, n_groups=n_groups),
        grid=(t // tm,),
        in_specs=[
            pl.BlockSpec((tm, d), colblk(0)), pl.BlockSpec((tm, d), colblk(1)),
            pl.BlockSpec((tm, d), colblk(2)), pl.BlockSpec((tm, d), colblk(3)),
            pl.BlockSpec((tm, d), row), pl.BlockSpec((tm, d), row),
            pl.BlockSpec((1, d), const), pl.BlockSpec((1, d), const),
            pl.BlockSpec(wsp.shape, lambda i: (0, 0, 0)), pl.BlockSpec((CHUNK, d), const),
            pl.BlockSpec((d, d), const), pl.BlockSpec((1, d), const),
        ],
        out_specs=[pl.BlockSpec((tm, d), row), pl.BlockSpec((tm, d), row)],
        out_shape=[jax.ShapeDtypeStruct((t, d), F32), jax.ShapeDtypeStruct((t, d), BF16)],
        scratch_shapes=[pltpu.VMEM((tm, d), BF16)],
        compiler_params=_params("parallel"),
        name="sgu_merge_out",
    )(zg, zg, zg, zg, oa, x, lng, lnb, wsp, bsp_full, wout, g2)


def _top_values(s, k):
    vals = []
    for _ in range(k):
        m = jnp.max(s, axis=0, keepdims=True)
        vals.append(m)
        s = jnp.where(s == m, NEG_INF, s)
    return vals


def _peer_topk_kernel(h2_ref, wpq_ref, keys_ref, s1_ref, s2_ref, p1_ref, p2_ref, thr_ref, *, n_heads):
    tm = h2_ref.shape[0]
    q = jnp.dot(h2_ref[...], wpq_ref[...], preferred_element_type=F32).astype(BF16)
    for h in range(n_heads):
        st = []
        for p in range(2):
            hp = 2 * h + p
            st.append(lax.dot_general(keys_ref[hp], q[:, hp * N_KEYS:(hp + 1) * N_KEYS],
                                      (((1,), (1,)), ((), ())), preferred_element_type=F32))
        v1 = _top_values(st[0], PEER_TOPK)
        v2 = _top_values(st[1], PEER_TOPK)
        v2_all = jnp.concatenate(v2, axis=0)
        v2_top8 = v2_all[:8]
        cand = [v1[0] + v2_all] + [v1[a] + v2_top8 for a in range(1, 8)]
        cand.append(jnp.concatenate(v1[8:], axis=0) + v2[0])
        tops = _top_values(jnp.concatenate(cand, axis=0), PEER_TOPK)
        best = tops[0]
        z = jnp.ones_like(best)
        for r in range(1, PEER_TOPK):
            z = z + jnp.exp(tops[r] - best)
        p1 = jnp.exp(st[0] - v1[0])
        p2 = jnp.exp(st[1] - v2[0]) * (1.0 / z)
        for c in range(tm // LANE):
            ls = slice(c * LANE, (c + 1) * LANE)
            tile = lambda a: a[:, ls].reshape(KEY_TILES, SUBLANE, LANE)
            s1_ref[h, c] = tile(st[0])
            s2_ref[h, c] = tile(st[1])
            p1_ref[h, c] = tile(p1)
            p2_ref[h, c] = tile(p2)
            thr_ref[h, c] = tops[PEER_TOPK - 1][:, ls]


def _peer_topk_call(h2, wpq, keys):
    t, d = h2.shape
    n_heads = keys.shape[0] // 2
    tm = _tile(t, 256)
    nc = tm // LANE
    tab = pl.BlockSpec((n_heads, nc, KEY_TILES, SUBLANE, LANE), lambda i: (0, i, 0, 0, 0))
    tab_shape = jax.ShapeDtypeStruct((n_heads, t // LANE, KEY_TILES, SUBLANE, LANE), F32)
    return pl.pallas_call(
        functools.partial(_peer_topk_kernel, n_heads=n_heads),
        grid=(t // tm,),
        in_specs=[pl.BlockSpec((tm, d), lambda i: (i, 0)), pl.BlockSpec(wpq.shape, lambda i: (0, 0)),
                  pl.BlockSpec(keys.shape, lambda i: (0, 0, 0))],
        out_specs=[tab, tab, tab, tab, pl.BlockSpec((n_heads, nc, 1, LANE), lambda i: (0, i, 0, 0))],
        out_shape=[tab_shape, tab_shape, tab_shape, tab_shape,
                   jax.ShapeDtypeStruct((n_heads, t // LANE, 1, LANE), F32)],
        compiler_params=_params("parallel"),
        name="peer_topk",
    )(h2, wpq, keys)


def _peer_dense_kernel(h2_ref, u_ref, vt_ref, s1_ref, p1_ref, s2_ref, p2_ref, thr_ref, x1_ref, gf_ref, o_ref,
                       acc_ref, *, n_heads, chains):
    g = pl.program_id(1)
    tm = h2_ref.shape[0]
    te = u_ref.shape[0] // chains
    ni = te // N_KEYS

    @pl.when(g == 0)
    def _():
        acc_ref[...] = jnp.zeros_like(acc_ref)

    h2 = h2_ref[...]
    for ch in range(chains):
        rows = slice(ch * te, (ch + 1) * te)
        st = lax.dot_general(u_ref[rows, :], h2, (((1,), (1,)), ((), ())), preferred_element_type=F32)
        row_blocks = []
        for il in range(ni):
            kr = ch * ni + il
            rs = slice(il * N_KEYS, (il + 1) * N_KEYS)
            lane_blocks = []
            for c in range(tm // LANE):
                ls = slice(c * LANE, (c + 1) * LANE)
                gate = jnp.zeros((KEY_TILES, SUBLANE, LANE), F32)
                for h in range(n_heads):
                    thr = jnp.broadcast_to(thr_ref[h, c], (SUBLANE, LANE))
                    s1b = jnp.broadcast_to(s1_ref[h, c, 0, kr:kr + 1, :], (SUBLANE, LANE))
                    p1b = jnp.broadcast_to(p1_ref[h, c, 0, kr:kr + 1, :], (SUBLANE, LANE))
                    sel = (s1b[None] + s2_ref[h, c]) >= thr[None]
                    gate = gate + jnp.where(sel, p1b[None] * p2_ref[h, c], 0.0)
                s = st[rs, ls].reshape(KEY_TILES, SUBLANE, LANE)
                lane_blocks.append((_gelu_tanh(s) * gate).reshape(N_KEYS, LANE).astype(BF16))
            row_blocks.append(jnp.concatenate(lane_blocks, axis=1))
        a = jnp.concatenate(row_blocks, axis=0)
        acc_ref[...] += jnp.dot(vt_ref[:, rows], a, preferred_element_type=F32)

    @pl.when(g == pl.num_programs(1) - 1)
    def _():
        y = x1_ref[...] + acc_ref[...].T
        o_ref[...] = _rms(y, gf_ref[...]).astype(o_ref.dtype)


def _peer_dense_call(h2, u, vt, s1, s2, p1, p2, thr, x1, gf):
    t, d = h2.shape
    n_experts = u.shape[0]
    n_heads = s1.shape[0]
    tm = _tile(t, 512)
    chains = 2
    te = SUBLANE * N_KEYS // chains
    nc = tm // LANE
    n_steps = n_experts // (chains * te)
    tok = lambda i, g: (i, 0)
    once = pl.Buffered(1)
    tab = pl.BlockSpec((n_heads, nc, KEY_TILES, SUBLANE, LANE), lambda i, g: (0, i, 0, 0, 0), pipeline_mode=once)
    key_tile = pl.BlockSpec((n_heads, nc, 1, SUBLANE, LANE), lambda i, g: (0, i, g, 0, 0))
    return pl.pallas_call(
        functools.partial(_peer_dense_kernel, n_heads=n_heads, chains=chains),
        grid=(t // tm, n_steps),
        in_specs=[
            pl.BlockSpec((tm, d), tok),
            pl.BlockSpec((chains * te, d), lambda i, g: (g, 0)),
            pl.BlockSpec((d, chains * te), lambda i, g: (0, g)),
            key_tile, key_tile, tab, tab,
            pl.BlockSpec((n_heads, nc, 1, LANE), lambda i, g: (0, i, 0, 0)),
            pl.BlockSpec((tm, d), tok, pipeline_mode=once), pl.BlockSpec((1, d), lambda i, g: (0, 0)),
        ],
        out_specs=pl.BlockSpec((tm, d), tok),
        out_shape=jax.ShapeDtypeStruct((t, d), F32),
        scratch_shapes=[pltpu.VMEM((d, tm), F32)],
        compiler_params=_params("parallel", "arbitrary"),
        name="peer_dense",
    )(h2, u, vt, s1, p1, s2, p2, thr, x1, gf)


def _swap_rope_halves(w):
    half = QK_ROPE // 2
    return jnp.concatenate([w[..., half:], w[..., :half]], axis=-1)


def _layer(x2d, pos, batch, seq, g_norm1, w_in, g_q_a, w_uq, g_kv_a, w_ukv, sgu_ln_g, sgu_ln_b, w_spatial,
           b_spatial, b_gate, w_out, g_norm2, w_peer_q, peer_keys, peer_u, peer_v, g_last):
    t, d = x2d.shape
    q_lora, kv_lora = g_q_a.shape[0], g_kv_a.shape[0]
    heads = w_uq.shape[1]
    off_kpe = q_lora + kv_lora
    off_sgu = off_kpe + QK_ROPE
    row = lambda a: a.reshape(1, -1)

    w_kpe = w_in[:, off_kpe:off_sgu]
    w_lat = jnp.concatenate([w_in[:, :off_sgu], _swap_rope_halves(w_kpe)], axis=1).astype(BF16)
    w_zg = w_in[:, off_sgu:].astype(BF16)
    zg_bias = jnp.concatenate([jnp.zeros((2 * d,), F32), b_gate]).reshape(1, -1)
    wq = jnp.transpose(w_uq, (1, 0, 2))
    wq = jnp.concatenate([wq, _swap_rope_halves(wq[..., QK_NOPE:])], axis=-1).astype(BF16)
    wkv = jnp.transpose(w_ukv, (1, 0, 2)).astype(BF16)
    freqs = ROPE_THETA ** (-jnp.arange(0, QK_ROPE, 2, dtype=F32) / QK_ROPE)
    freq = jnp.concatenate([freqs, freqs]).reshape(1, -1)
    sign = jnp.concatenate([-jnp.ones((QK_ROPE // 2,), F32), jnp.ones((QK_ROPE // 2,), F32)]).reshape(1, -1)
    bsp_full = jnp.repeat(b_spatial.T, d // b_spatial.shape[0], axis=1)
    n_peer_heads = w_peer_q.shape[1]
    wpq = w_peer_q.reshape(d, -1).astype(BF16)
    keys = peer_keys.reshape(2 * n_peer_heads, N_KEYS, -1).astype(BF16)
    u = peer_u.astype(BF16)
    vt = peer_v.T.astype(BF16)

    h = _norm_call(x2d, row(g_norm1))
    cq, ckv, kpe, cos, sin = _latent_call(h, w_lat, row(g_q_a), row(g_kv_a), pos, freq, sign)
    scale = (QK_NOPE + QK_ROPE) ** -0.5 * math.log2(math.e)
    q, k, v = _head_proj_calls(cq, ckv, wq, wkv, kpe, cos, sin, batch, seq, scale)
    oa = _flash_call(q, k, v).reshape(t, heads * V_HEAD)
    zg = _zg_call(h, w_zg, zg_bias, 2 * d)
    x1, h2 = _mix_call(zg, oa, x2d, row(sgu_ln_g), row(sgu_ln_b), w_spatial, bsp_full, w_out.astype(BF16),
                       row(g_norm2))
    s1, s2, p1, p2, thr = _peer_topk_call(h2, wpq, keys)
    return _peer_dense_call(h2, u, vt, s1, s2, p1, p2, thr, x1, row(g_last))


def kernel(x, positions, g_norm1, w_in, g_q_a, w_uq, g_kv_a, w_ukv, sgu_ln_g, sgu_ln_b, w_spatial, b_spatial,
           b_gate, w_out, g_norm2, w_peer_q, peer_keys, peer_u, peer_v, g_final):
    batch, seq, d = x.shape
    depth = g_norm1.shape[0]
    assert depth == 1, "the final norm is fused into the last layer's expert kernel"
    x2d = x.reshape(batch * seq, d)
    pos = positions.astype(F32).reshape(batch * seq, 1)
    out = _layer(x2d, pos, batch, seq, g_norm1[0], w_in[0], g_q_a[0], w_uq[0], g_kv_a[0], w_ukv[0], sgu_ln_g[0],
                 sgu_ln_b[0], w_spatial[0], b_spatial[0], b_gate[0], w_out[0], g_norm2[0], w_peer_q[0],
                 peer_keys[0], peer_u[0], peer_v[0], g_final)
    return out.reshape(batch, seq, d)
```

```python
import functools
import math

import jax
import jax.numpy as jnp
from jax import lax
from jax.experimental import pallas as pl
from jax.experimental.pallas import tpu as pltpu

BF16 = jnp.bfloat16
F32 = jnp.float32

EPS = 1e-6
ROPE_THETA = 10000.0
QK_NOPE = 128
QK_ROPE = 64
V_HEAD = 128
QK_PAD = 256
CHUNK = 128
N_KEYS = 128
PEER_TOPK = 16
LANE = 128
SUBLANE = 8
KEY_TILES = N_KEYS // SUBLANE
VMEM_LIMIT_BYTES = 56 * 1024 * 1024
NEG_INF = float("-inf")
FLASH_TQ = 1024
FLASH_TK = 1024
FLASH_HEADS_PER_STEP = 4


def _params(*sem, flags=None):
    return pltpu.CompilerParams(dimension_semantics=sem, vmem_limit_bytes=VMEM_LIMIT_BYTES, flags=flags)


def _tile(n, pref):
    t = min(n, pref)
    assert n % t == 0, (n, t)
    return t


def _gelu_tanh(x):
    c = math.sqrt(2.0 / math.pi)
    return x * (0.5 * (1.0 + jnp.tanh(c * (x + 0.044715 * (x * x * x)))))


def _rms(x, g):
    r = lax.rsqrt(jnp.mean(x * x, axis=-1, keepdims=True) + EPS)
    return (x * r) * g


def _latent_kernel(x_ref, g1_ref, w_ref, gq_ref, gkv_ref, pos_ref, freq_ref, sign_ref,
                   h_ref, cq_ref, ckv_ref, kpe_ref, cos_ref, sin_ref, *, q_lora, kv_lora):
    h = _rms(x_ref[...], g1_ref[...]).astype(h_ref.dtype)
    h_ref[...] = h
    p = jnp.dot(h, w_ref[...], preferred_element_type=F32)
    o = q_lora + kv_lora
    cq_ref[...] = _rms(p[:, :q_lora], gq_ref[...]).astype(cq_ref.dtype)
    ckv_ref[...] = _rms(p[:, q_lora:o], gkv_ref[...]).astype(ckv_ref.dtype)
    ang = pos_ref[...] * freq_ref[...]
    c = jnp.cos(ang)
    s = jnp.sin(ang) * sign_ref[...]
    cos_ref[...] = c
    sin_ref[...] = s
    kr = p[:, o:o + QK_ROPE] * c + p[:, o + QK_ROPE:o + 2 * QK_ROPE] * s
    kpe_ref[...] = jnp.concatenate([kr, jnp.zeros_like(kr)], axis=-1).astype(kpe_ref.dtype)


def _latent_call(x, g1, w_lat, gq, gkv, pos, freq, sign):
    t, d = x.shape
    q_lora, kv_lora = gq.shape[1], gkv.shape[1]
    n = w_lat.shape[1]
    tm = _tile(t, 512)
    row = lambda i: (i, 0)
    const = lambda i: (0, 0)
    return pl.pallas_call(
        functools.partial(_latent_kernel, q_lora=q_lora, kv_lora=kv_lora),
        grid=(t // tm,),
        in_specs=[
            pl.BlockSpec((tm, d), row), pl.BlockSpec((1, d), const), pl.BlockSpec((d, n), const),
            pl.BlockSpec((1, q_lora), const), pl.BlockSpec((1, kv_lora), const),
            pl.BlockSpec((tm, 1), row), pl.BlockSpec((1, QK_ROPE), const), pl.BlockSpec((1, QK_ROPE), const),
        ],
        out_specs=[
            pl.BlockSpec((tm, d), row), pl.BlockSpec((tm, q_lora), row), pl.BlockSpec((tm, kv_lora), row),
            pl.BlockSpec((tm, 2 * QK_ROPE), row), pl.BlockSpec((tm, QK_ROPE), row), pl.BlockSpec((tm, QK_ROPE), row),
        ],
        out_shape=[
            jax.ShapeDtypeStruct((t, d), BF16),
            jax.ShapeDtypeStruct((t, q_lora), BF16), jax.ShapeDtypeStruct((t, kv_lora), BF16),
            jax.ShapeDtypeStruct((t, 2 * QK_ROPE), BF16),
            jax.ShapeDtypeStruct((t, QK_ROPE), F32), jax.ShapeDtypeStruct((t, QK_ROPE), F32),
        ],
        compiler_params=_params("parallel"),
        name="norm1_latent",
    )(x, g1, w_lat, gq, gkv, pos, freq, sign)


def _head_proj_kernel(cq_ref, ckv_ref, wq_ref, wkv_ref, kpe_ref, cos_ref, sin_ref, q_ref, k_ref, v_ref, *, scale):
    cq = cq_ref[...]
    ckv = ckv_ref[...]
    kpe = kpe_ref[...]
    cos = cos_ref[...]
    sin = sin_ref[...]
    for g in range(wq_ref.shape[0]):
        r = jnp.dot(cq, wq_ref[g], preferred_element_type=F32)
        qr = r[:, QK_NOPE:QK_NOPE + QK_ROPE] * cos + r[:, QK_NOPE + QK_ROPE:] * sin
        q = jnp.concatenate([r[:, :QK_NOPE], qr, jnp.zeros_like(qr)], axis=-1) * scale
        q_ref[0, g] = q.astype(q_ref.dtype)
        r = jnp.dot(ckv, wkv_ref[g], preferred_element_type=F32)
        k_ref[0, g] = jnp.concatenate([r[:, :QK_NOPE].astype(k_ref.dtype), kpe], axis=-1)
        v_ref[0, g] = r[:, QK_NOPE:].astype(v_ref.dtype)


def _head_proj_calls(cq, ckv, wq, wkv, kpe, cos, sin, batch, seq, scale):
    t = cq.shape[0]
    heads = wq.shape[0]
    tm = _tile(seq, 512)
    group = _tile(heads, 8)
    spb = seq // tm
    row = lambda i, h: (i, 0)
    whead = lambda i, h: (h, 0, 0)
    out = lambda i, h: (i // spb, h, i % spb, 0)
    return pl.pallas_call(
        functools.partial(_head_proj_kernel, scale=scale),
        grid=(t // tm, heads // group),
        in_specs=[pl.BlockSpec((tm, cq.shape[1]), row), pl.BlockSpec((tm, ckv.shape[1]), row),
                  pl.BlockSpec((group,) + wq.shape[1:], whead), pl.BlockSpec((group,) + wkv.shape[1:], whead),
                  pl.BlockSpec((tm, 2 * QK_ROPE), row), pl.BlockSpec((tm, QK_ROPE), row),
                  pl.BlockSpec((tm, QK_ROPE), row)],
        out_specs=[pl.BlockSpec((1, group, tm, QK_PAD), out), pl.BlockSpec((1, group, tm, QK_PAD), out),
                   pl.BlockSpec((1, group, tm, V_HEAD), out)],
        out_shape=[jax.ShapeDtypeStruct((batch, heads, seq, QK_PAD), BF16),
                   jax.ShapeDtypeStruct((batch, heads, seq, QK_PAD), BF16),
                   jax.ShapeDtypeStruct((batch, heads, seq, V_HEAD), BF16)],
        compiler_params=_params("parallel", "arbitrary"),
        name="head_proj",
    )(cq, ckv, wq, wkv, kpe, cos, sin)


def _flash_kernel(qi_tab, ki_tab, q_ref, k_ref, v_ref, o_ref, m_ref, l_ref, acc_ref, *, tq, tk, group):
    t = pl.program_id(2)
    qi = qi_tab[t]
    ki = ki_tab[t]

    @pl.when(ki == 0)
    def _():
        m_ref[...] = jnp.full_like(m_ref, NEG_INF)
        l_ref[...] = jnp.zeros_like(l_ref)
        acc_ref[...] = jnp.zeros_like(acc_ref)

    def chain(g, rq, ck, masked):
        s = lax.dot_general(q_ref[0, g, rq, :], k_ref[0, g, ck, :], (((1,), (1,)), ((), ())),
                            preferred_element_type=F32)
        if masked:
            row = qi * tq + rq.start + lax.broadcasted_iota(jnp.int32, s.shape, 0)
            col = ki * tk + ck.start + lax.broadcasted_iota(jnp.int32, s.shape, 1)
            s = jnp.where(row >= col, s, NEG_INF)
        m_prev = m_ref[g, rq, :]
        m_new = jnp.maximum(m_prev, jnp.max(s, axis=-1, keepdims=True))
        alpha = jnp.exp2(m_prev - m_new)
        ps = [jnp.exp2(s[:, j * LANE:(j + 1) * LANE] - m_new) for j in range(s.shape[1] // LANE)]
        l_ref[g, rq, :] = alpha * l_ref[g, rq, :] + functools.reduce(lambda a, b: a + b, ps)
        p = jnp.concatenate(ps, axis=-1).astype(v_ref.dtype)
        acc_ref[g, rq, :] = alpha * acc_ref[g, rq, :] + jnp.dot(p, v_ref[0, g, ck, :],
                                                                preferred_element_type=F32)
        m_ref[g, rq, :] = m_new

    def update(masked):
        for g in range(group):
            if masked and tq == tk and tq % (2 * LANE) == 0:
                half = tq // 2
                chain(g, slice(0, half), slice(0, half), True)
                chain(g, slice(half, tq), slice(0, tk), True)
            else:
                chain(g, slice(0, tq), slice(0, tk), masked)

    crosses_diagonal = (ki + 1) * tk - 1 > qi * tq

    @pl.when(jnp.logical_not(crosses_diagonal))
    def _():
        update(False)

    @pl.when(crosses_diagonal)
    def _():
        update(True)

    @pl.when(ki == ((qi + 1) * tq - 1) // tk)
    def _():
        for g in range(group):
            l = jnp.sum(l_ref[g], axis=-1, keepdims=True)
            o_ref[0, :, g * V_HEAD:(g + 1) * V_HEAD] = (acc_ref[g] / l).astype(o_ref.dtype)


def _flash_call(q, k, v):
    batch, heads, seq, _ = q.shape
    tq = _tile(seq, FLASH_TQ)
    tk = _tile(seq, FLASH_TK)
    group = _tile(heads, FLASH_HEADS_PER_STEP)
    pairs = [(i, j) for i in range(seq // tq) for j in range(((i + 1) * tq - 1) // tk + 1)]
    qi_tab = jnp.asarray([p[0] for p in pairs], jnp.int32)
    ki_tab = jnp.asarray([p[1] for p in pairs], jnp.int32)
    qmap = lambda b, h, t, qt, kt: (b, h, qt[t], 0)
    kmap = lambda b, h, t, qt, kt: (b, h, kt[t], 0)
    return pl.pallas_call(
        functools.partial(_flash_kernel, tq=tq, tk=tk, group=group),
        grid_spec=pltpu.PrefetchScalarGridSpec(
            num_scalar_prefetch=2,
            grid=(batch, heads // group, len(pairs)),
            in_specs=[pl.BlockSpec((1, group, tq, QK_PAD), qmap), pl.BlockSpec((1, group, tk, QK_PAD), kmap),
                      pl.BlockSpec((1, group, tk, V_HEAD), kmap)],
            out_specs=pl.BlockSpec((1, tq, group * V_HEAD), lambda b, h, t, qt, kt: (b, qt[t], h)),
            scratch_shapes=[pltpu.VMEM((group, tq, LANE), F32), pltpu.VMEM((group, tq, LANE), F32),
                            pltpu.VMEM((group, tq, V_HEAD), F32)],
        ),
        out_shape=jax.ShapeDtypeStruct((batch, seq, heads * V_HEAD), BF16),
        compiler_params=_params("parallel", "parallel", "arbitrary"),
        name="flash_attn",
    )(qi_tab, ki_tab, q, k, v)


def _zg_kernel(h_ref, w_ref, b_ref, o_ref, *, n_gelu_blocks):
    j = pl.program_id(0)
    n_sub = 4
    sub = h_ref.shape[0] // n_sub

    def run(epilogue):
        for r in range(n_sub):
            rs = slice(r * sub, (r + 1) * sub)
            z = jnp.dot(h_ref[rs, :], w_ref[...], preferred_element_type=F32)
            o_ref[rs, :] = epilogue(z).astype(o_ref.dtype)

    @pl.when(j < n_gelu_blocks)
    def _():
        run(_gelu_tanh)

    @pl.when(j >= n_gelu_blocks)
    def _():
        run(lambda z: 1.0 / (1.0 + jnp.exp(-(z + b_ref[...]))))


def _zg_call(h, w_zg, bias, n_gelu_cols):
    t, d = h.shape
    n = w_zg.shape[1]
    tm = _tile(t, 1024)
    tn = _tile(n_gelu_cols, 1024)
    return pl.pallas_call(
        functools.partial(_zg_kernel, n_gelu_blocks=n_gelu_cols // tn),
        grid=(n // tn, t // tm),
        in_specs=[pl.BlockSpec((tm, d), lambda j, i: (i, 0)), pl.BlockSpec((d, tn), lambda j, i: (0, j)),
                  pl.BlockSpec((1, tn), lambda j, i: (0, j))],
        out_specs=pl.BlockSpec((tm, tn), lambda j, i: (i, j)),
        out_shape=jax.ShapeDtypeStruct((t, n), BF16),
        compiler_params=_params("parallel", "parallel"),
        name="sgu_gate_proj",
    )(h, w_zg, bias)


def _mix_kernel(u_ref, v_ref, ga_ref, gb_ref, oa_ref, x_ref, lng_ref, lnb_ref, wsp_ref, bsp_ref, wout_ref,
                g2_ref, x1_ref, h2_ref, merged_ref, *, n_groups):
    tm, width = u_ref.shape
    gd = width // n_groups
    v = v_ref[...].astype(F32)
    mu = jnp.mean(v, axis=-1, keepdims=True)
    dv = v - mu
    var = jnp.mean(dv * dv, axis=-1, keepdims=True)
    vn = ((dv * lax.rsqrt(var + EPS)) * lng_ref[...] + lnb_ref[...]).astype(BF16)
    row = lax.broadcasted_iota(jnp.int32, (CHUNK, CHUNK), 0)
    col = lax.broadcasted_iota(jnp.int32, (CHUNK, CHUNK), 1)
    causal = row >= col
    for g in range(n_groups):
        w = jnp.where(causal, wsp_ref[g], 0.0).astype(BF16)
        cs = slice(g * gd, (g + 1) * gd)
        for c in range(tm // CHUNK):
            rs = slice(c * CHUNK, (c + 1) * CHUNK)
            mixed = jnp.dot(w, vn[rs, cs], preferred_element_type=F32) + bsp_ref[:, cs]
            ob = u_ref[rs, cs].astype(F32) * mixed
            merged = ga_ref[rs, cs].astype(F32) * oa_ref[rs, cs].astype(F32) + gb_ref[rs, cs].astype(F32) * ob
            merged_ref[rs, cs] = merged.astype(merged_ref.dtype)
    y = x_ref[...] + jnp.dot(merged_ref[...], wout_ref[...], preferred_element_type=F32)
    x1_ref[...] = y
    h2_ref[...] = _rms(y, g2_ref[...]).astype(h2_ref.dtype)


def _mix_call(zg, oa, x, lng, lnb, wsp, bsp_full, wout, g2):
    t, d = x.shape
    n_groups = wsp.shape[0]
    tm = _tile(t, 256)
    row = lambda i: (i, 0)
    const = lambda i: (0, 0)
    colblk = lambda c: (lambda i: (i, c))
    return pl.pallas_call(
        functools.partial(_mix_kernel, n_groups=n_groups),
        grid=(t // tm,),
        in_specs=[
            pl.BlockSpec((tm, d), colblk(0)), pl.BlockSpec((tm, d), colblk(1)),
            pl.BlockSpec((tm, d), colblk(2)), pl.BlockSpec((tm, d), colblk(3)),
            pl.BlockSpec((tm, d), row), pl.BlockSpec((tm, d), row),
            pl.BlockSpec((1, d), const), pl.BlockSpec((1, d), const),
            pl.BlockSpec(wsp.shape, lambda i: (0, 0, 0)), pl.BlockSpec((CHUNK, d), const),
            pl.BlockSpec((d, d), const), pl.BlockSpec((1, d), const),
        ],
        out_specs=[pl.BlockSpec((tm, d), row), pl.BlockSpec((tm, d), row)],
        out_shape=[jax.ShapeDtypeStruct((t, d), F32), jax.ShapeDtypeStruct((t, d), BF16)],
        scratch_shapes=[pltpu.VMEM((tm, d), BF16)],
        compiler_params=_params("parallel"),
        name="sgu_merge_out",
    )(zg, zg, zg, zg, oa, x, lng, lnb, wsp, bsp_full, wout, g2)


def _top_values(s, k):
    vals = []
    for _ in range(k):
        m = jnp.max(s, axis=0, keepdims=True)
        vals.append(m)
        s = jnp.where(s == m, NEG_INF, s)
    return vals


def _peer_topk_kernel(h2_ref, wpq_ref, keys_ref, s1_ref, s2_ref, p1_ref, p2_ref, thr_ref, *, n_heads):
    tm = h2_ref.shape[0]
    q = jnp.dot(h2_ref[...], wpq_ref[...], preferred_element_type=F32).astype(BF16)
    for h in range(n_heads):
        st = []
        for p in range(2):
            hp = 2 * h + p
            st.append(lax.dot_general(keys_ref[hp], q[:, hp * N_KEYS:(hp + 1) * N_KEYS],
                                      (((1,), (1,)), ((), ())), preferred_element_type=F32))
        v1 = _top_values(st[0], PEER_TOPK)
        v2 = _top_values(st[1], PEER_TOPK)
        v2_all = jnp.concatenate(v2, axis=0)
        v2_top8 = v2_all[:8]
        cand = [v1[0] + v2_all] + [v1[a] + v2_top8 for a in range(1, 8)]
        cand.append(jnp.concatenate(v1[8:], axis=0) + v2[0])
        tops = _top_values(jnp.concatenate(cand, axis=0), PEER_TOPK)
        best = tops[0]
        z = jnp.ones_like(best)
        for r in range(1, PEER_TOPK):
            z = z + jnp.exp(tops[r] - best)
        p1 = jnp.exp(st[0] - v1[0])
        p2 = jnp.exp(st[1] - v2[0]) * (1.0 / z)
        for c in range(tm // LANE):
            ls = slice(c * LANE, (c + 1) * LANE)
            tile = lambda a: a[:, ls].reshape(KEY_TILES, SUBLANE, LANE)
            s1_ref[h, c] = tile(st[0])
            s2_ref[h, c] = tile(st[1])
            p1_ref[h, c] = tile(p1)
            p2_ref[h, c] = tile(p2)
            thr_ref[h, c] = tops[PEER_TOPK - 1][:, ls]


def _peer_topk_call(h2, wpq, keys):
    t, d = h2.shape
    n_heads = keys.shape[0] // 2
    tm = _tile(t, 256)
    nc = tm // LANE
    tab = pl.BlockSpec((n_heads, nc, KEY_TILES, SUBLANE, LANE), lambda i: (0, i, 0, 0, 0))
    tab_shape = jax.ShapeDtypeStruct((n_heads, t // LANE, KEY_TILES, SUBLANE, LANE), F32)
    return pl.pallas_call(
        functools.partial(_peer_topk_kernel, n_heads=n_heads),
        grid=(t // tm,),
        in_specs=[pl.BlockSpec((tm, d), lambda i: (i, 0)), pl.BlockSpec(wpq.shape, lambda i: (0, 0)),
                  pl.BlockSpec(keys.shape, lambda i: (0, 0, 0))],
        out_specs=[tab, tab, tab, tab, pl.BlockSpec((n_heads, nc, 1, LANE), lambda i: (0, i, 0, 0))],
        out_shape=[tab_shape, tab_shape, tab_shape, tab_shape,
                   jax.ShapeDtypeStruct((n_heads, t // LANE, 1, LANE), F32)],
        compiler_params=_params("parallel"),
        name="peer_topk",
    )(h2, wpq, keys)


def _peer_dense_kernel(h2_ref, u_ref, vt_ref, s1_ref, p1_ref, s2_ref, p2_ref, thr_ref, x1_ref, gf_ref, o_ref,
                       acc_ref, *, n_heads, chains):
    g = pl.program_id(1)
    tm = h2_ref.shape[0]
    te = u_ref.shape[0] // chains
    ni = te // N_KEYS

    @pl.when(g == 0)
    def _():
        acc_ref[...] = jnp.zeros_like(acc_ref)

    h2 = h2_ref[...]
    for ch in range(chains):
        rows = slice(ch * te, (ch + 1) * te)
        st = lax.dot_general(u_ref[rows, :], h2, (((1,), (1,)), ((), ())), preferred_element_type=F32)
        row_blocks = []
        for il in range(ni):
            kr = ch * ni + il
            rs = slice(il * N_KEYS, (il + 1) * N_KEYS)
            lane_blocks = []
            for c in range(tm // LANE):
                ls = slice(c * LANE, (c + 1) * LANE)
                gate = jnp.zeros((KEY_TILES, SUBLANE, LANE), F32)
                for h in range(n_heads):
                    thr = jnp.broadcast_to(thr_ref[h, c], (SUBLANE, LANE))
                    s1b = jnp.broadcast_to(s1_ref[h, c, 0, kr:kr + 1, :], (SUBLANE, LANE))
                    p1b = jnp.broadcast_to(p1_ref[h, c, 0, kr:kr + 1, :], (SUBLANE, LANE))
                    sel = (s1b[None] + s2_ref[h, c]) >= thr[None]
                    gate = gate + jnp.where(sel, p1b[None] * p2_ref[h, c], 0.0)
                s = st[rs, ls].reshape(KEY_TILES, SUBLANE, LANE)
                lane_blocks.append((_gelu_tanh(s) * gate).reshape(N_KEYS, LANE).astype(BF16))
            row_blocks.append(jnp.concatenate(lane_blocks, axis=1))
        a = jnp.concatenate(row_blocks, axis=0)
        acc_ref[...] += jnp.dot(vt_ref[:, rows], a, preferred_element_type=F32)

    @pl.when(g == pl.num_programs(1) - 1)
    def _():
        y = x1_ref[...] + acc_ref[...].T
        o_ref[...] = _rms(y, gf_ref[...]).astype(o_ref.dtype)


def _peer_dense_call(h2, u, vt, s1, s2, p1, p2, thr, x1, gf):
    t, d = h2.shape
    n_experts = u.shape[0]
    n_heads = s1.shape[0]
    tm = _tile(t, 512)
    chains = 2
    te = SUBLANE * N_KEYS // chains
    nc = tm // LANE
    n_steps = n_experts // (chains * te)
    tok = lambda i, g: (i, 0)
    once = pl.Buffered(1)
    tab = pl.BlockSpec((n_heads, nc, KEY_TILES, SUBLANE, LANE), lambda i, g: (0, i, 0, 0, 0), pipeline_mode=once)
    key_tile = pl.BlockSpec((n_heads, nc, 1, SUBLANE, LANE), lambda i, g: (0, i, g, 0, 0))
    return pl.pallas_call(
        functools.partial(_peer_dense_kernel, n_heads=n_heads, chains=chains),
        grid=(t // tm, n_steps),
        in_specs=[
            pl.BlockSpec((tm, d), tok),
            pl.BlockSpec((chains * te, d), lambda i, g: (g, 0)),
            pl.BlockSpec((d, chains * te), lambda i, g: (0, g)),
            key_tile, key_tile, tab, tab,
            pl.BlockSpec((n_heads, nc, 1, LANE), lambda i, g: (0, i, 0, 0)),
            pl.BlockSpec((tm, d), tok, pipeline_mode=once), pl.BlockSpec((1, d), lambda i, g: (0, 0)),
        ],
        out_specs=pl.BlockSpec((tm, d), tok),
        out_shape=jax.ShapeDtypeStruct((t, d), F32),
        scratch_shapes=[pltpu.VMEM((d, tm), F32)],
        compiler_params=_params("parallel", "arbitrary"),
        name="peer_dense",
    )(h2, u, vt, s1, p1, s2, p2, thr, x1, gf)


def _swap_rope_halves(w):
    half = QK_ROPE // 2
    return jnp.concatenate([w[..., half:], w[..., :half]], axis=-1)


def _layer(x2d, pos, batch, seq, g_norm1, w_in, g_q_a, w_uq, g_kv_a, w_ukv, sgu_ln_g, sgu_ln_b, w_spatial,
           b_spatial, b_gate, w_out, g_norm2, w_peer_q, peer_keys, peer_u, peer_v, g_last):
    t, d = x2d.shape
    q_lora, kv_lora = g_q_a.shape[0], g_kv_a.shape[0]
    heads = w_uq.shape[1]
    off_kpe = q_lora + kv_lora
    off_sgu = off_kpe + QK_ROPE
    row = lambda a: a.reshape(1, -1)

    w_kpe = w_in[:, off_kpe:off_sgu]
    w_lat = jnp.concatenate([w_in[:, :off_sgu], _swap_rope_halves(w_kpe)], axis=1).astype(BF16)
    w_zg = w_in[:, off_sgu:].astype(BF16)
    zg_bias = jnp.concatenate([jnp.zeros((2 * d,), F32), b_gate]).reshape(1, -1)
    wq = jnp.transpose(w_uq, (1, 0, 2))
    wq = jnp.concatenate([wq, _swap_rope_halves(wq[..., QK_NOPE:])], axis=-1).astype(BF16)
    wkv = jnp.transpose(w_ukv, (1, 0, 2)).astype(BF16)
    freqs = ROPE_THETA ** (-jnp.arange(0, QK_ROPE, 2, dtype=F32) / QK_ROPE)
    freq = jnp.concatenate([freqs, freqs]).reshape(1, -1)
    sign = jnp.concatenate([-jnp.ones((QK_ROPE // 2,), F32), jnp.ones((QK_ROPE // 2,), F32)]).reshape(1, -1)
    bsp_full = jnp.repeat(b_spatial.T, d // b_spatial.shape[0], axis=1)
    n_peer_heads = w_peer_q.shape[1]
    wpq = w_peer_q.reshape(d, -1).astype(BF16)
    keys = peer_keys.reshape(2 * n_peer_heads, N_KEYS, -1).astype(BF16)
    u = peer_u.astype(BF16)
    vt = peer_v.T.astype(BF16)

    h, cq, ckv, kpe, cos, sin = _latent_call(x2d, row(g_norm1), w_lat, row(g_q_a), row(g_kv_a), pos, freq, sign)
    scale = (QK_NOPE + QK_ROPE) ** -0.5 * math.log2(math.e)
    q, k, v = _head_proj_calls(cq, ckv, wq, wkv, kpe, cos, sin, batch, seq, scale)
    oa = _flash_call(q, k, v).reshape(t, heads * V_HEAD)
    zg = _zg_call(h, w_zg, zg_bias, 2 * d)
    x1, h2 = _mix_call(zg, oa, x2d, row(sgu_ln_g), row(sgu_ln_b), w_spatial, bsp_full, w_out.astype(BF16),
                       row(g_norm2))
    s1, s2, p1, p2, thr = _peer_topk_call(h2, wpq, keys)
    return _peer_dense_call(h2, u, vt, s1, s2, p1, p2, thr, x1, row(g_last))


def kernel(x, positions, g_norm1, w_in, g_q_a, w_uq, g_kv_a, w_ukv, sgu_ln_g, sgu_ln_b, w_spatial, b_spatial,
           b_gate, w_out, g_norm2, w_peer_q, peer_keys, peer_u, peer_v, g_final):
    batch, seq, d = x.shape
    depth = g_norm1.shape[0]
    assert depth == 1, "the final norm is fused into the last layer's expert kernel"
    x2d = x.reshape(batch * seq, d)
    pos = positions.astype(F32).reshape(batch * seq, 1)
    out = _layer(x2d, pos, batch, seq, g_norm1[0], w_in[0], g_q_a[0], w_uq[0], g_kv_a[0], w_ukv[0], sgu_ln_g[0],
                 sgu_ln_b[0], w_spatial[0], b_spatial[0], b_gate[0], w_out[0], g_norm2[0], w_peer_q[0],
                 peer_keys[0], peer_u[0], peer_v[0], g_final)
    return out.reshape(batch, seq, d)
```

```python
import functools
import math

import jax
import jax.numpy as jnp
from jax import lax
from jax.experimental import pallas as pl
from jax.experimental.pallas import tpu as pltpu

BF16 = jnp.bfloat16
F32 = jnp.float32

EPS = 1e-6
ROPE_THETA = 10000.0
QK_NOPE = 128
QK_ROPE = 64
V_HEAD = 128
QK_PAD = 256
CHUNK = 128
N_KEYS = 128
PEER_TOPK = 16
LANE = 128
SUBLANE = 8
KEY_TILES = N_KEYS // SUBLANE
VMEM_LIMIT_BYTES = 56 * 1024 * 1024
NEG_INF = float("-inf")
FLASH_TQ = 1024
FLASH_TK = 1024
FLASH_HEADS_PER_STEP = 4


def _params(*sem, flags=None):
    return pltpu.CompilerParams(dimension_semantics=sem, vmem_limit_bytes=VMEM_LIMIT_BYTES, flags=flags)


def _tile(n, pref):
    t = min(n, pref)
    assert n % t == 0, (n, t)
    return t


def _gelu_tanh(x):
    c = math.sqrt(2.0 / math.pi)
    return x * (0.5 * (1.0 + jnp.tanh(c * (x + 0.044715 * (x * x * x)))))


def _rms(x, g):
    r = lax.rsqrt(jnp.mean(x * x, axis=-1, keepdims=True) + EPS)
    return (x * r) * g


def _latent_kernel(x_ref, g1_ref, w_ref, gq_ref, gkv_ref, pos_ref, freq_ref, sign_ref,
                   h_ref, cq_ref, ckv_ref, kpe_ref, cos_ref, sin_ref, *, q_lora, kv_lora):
    h = _rms(x_ref[...], g1_ref[...]).astype(h_ref.dtype)
    h_ref[...] = h
    p = jnp.dot(h, w_ref[...], preferred_element_type=F32)
    o = q_lora + kv_lora
    cq_ref[...] = _rms(p[:, :q_lora], gq_ref[...]).astype(cq_ref.dtype)
    ckv_ref[...] = _rms(p[:, q_lora:o], gkv_ref[...]).astype(ckv_ref.dtype)
    ang = pos_ref[...] * freq_ref[...]
    c = jnp.cos(ang)
    s = jnp.sin(ang) * sign_ref[...]
    cos_ref[...] = c
    sin_ref[...] = s
    kr = p[:, o:o + QK_ROPE] * c + p[:, o + QK_ROPE:o + 2 * QK_ROPE] * s
    kpe_ref[...] = jnp.concatenate([kr, jnp.zeros_like(kr)], axis=-1).astype(kpe_ref.dtype)


def _latent_call(x, g1, w_lat, gq, gkv, pos, freq, sign):
    t, d = x.shape
    q_lora, kv_lora = gq.shape[1], gkv.shape[1]
    n = w_lat.shape[1]
    tm = _tile(t, 512)
    row = lambda i: (i, 0)
    const = lambda i: (0, 0)
    return pl.pallas_call(
        functools.partial(_latent_kernel, q_lora=q_lora, kv_lora=kv_lora),
        grid=(t // tm,),
        in_specs=[
            pl.BlockSpec((tm, d), row), pl.BlockSpec((1, d), const), pl.BlockSpec((d, n), const),
            pl.BlockSpec((1, q_lora), const), pl.BlockSpec((1, kv_lora), const),
            pl.BlockSpec((tm, 1), row), pl.BlockSpec((1, QK_ROPE), const), pl.BlockSpec((1, QK_ROPE), const),
        ],
        out_specs=[
            pl.BlockSpec((tm, d), row), pl.BlockSpec((tm, q_lora), row), pl.BlockSpec((tm, kv_lora), row),
            pl.BlockSpec((tm, 2 * QK_ROPE), row), pl.BlockSpec((tm, QK_ROPE), row), pl.BlockSpec((tm, QK_ROPE), row),
        ],
        out_shape=[
            jax.ShapeDtypeStruct((t, d), BF16),
            jax.ShapeDtypeStruct((t, q_lora), BF16), jax.ShapeDtypeStruct((t, kv_lora), BF16),
            jax.ShapeDtypeStruct((t, 2 * QK_ROPE), BF16),
            jax.ShapeDtypeStruct((t, QK_ROPE), F32), jax.ShapeDtypeStruct((t, QK_ROPE), F32),
        ],
        compiler_params=_params("parallel"),
        name="norm1_latent",
    )(x, g1, w_lat, gq, gkv, pos, freq, sign)


def _head_proj_kernel(cq_ref, ckv_ref, wq_ref, wkv_ref, kpe_ref, cos_ref, sin_ref, q_ref, k_ref, v_ref, *, scale):
    cq = cq_ref[...]
    ckv = ckv_ref[...]
    kpe = kpe_ref[...]
    cos = cos_ref[...]
    sin = sin_ref[...]
    for g in range(wq_ref.shape[0]):
        r = jnp.dot(cq, wq_ref[g], preferred_element_type=F32)
        qr = r[:, QK_NOPE:QK_NOPE + QK_ROPE] * cos + r[:, QK_NOPE + QK_ROPE:] * sin
        q = jnp.concatenate([r[:, :QK_NOPE], qr, jnp.zeros_like(qr)], axis=-1) * scale
        q_ref[0, g] = q.astype(q_ref.dtype)
        r = jnp.dot(ckv, wkv_ref[g], preferred_element_type=F32)
        k_ref[0, g] = jnp.concatenate([r[:, :QK_NOPE].astype(k_ref.dtype), kpe], axis=-1)
        v_ref[0, g] = r[:, QK_NOPE:].astype(v_ref.dtype)


def _head_proj_calls(cq, ckv, wq, wkv, kpe, cos, sin, batch, seq, scale):
    t = cq.shape[0]
    heads = wq.shape[0]
    tm = _tile(seq, 512)
    group = _tile(heads, 8)
    spb = seq // tm
    row = lambda i, h: (i, 0)
    whead = lambda i, h: (h, 0, 0)
    out = lambda i, h: (i // spb, h, i % spb, 0)
    return pl.pallas_call(
        functools.partial(_head_proj_kernel, scale=scale),
        grid=(t // tm, heads // group),
        in_specs=[pl.BlockSpec((tm, cq.shape[1]), row), pl.BlockSpec((tm, ckv.shape[1]), row),
                  pl.BlockSpec((group,) + wq.shape[1:], whead), pl.BlockSpec((group,) + wkv.shape[1:], whead),
                  pl.BlockSpec((tm, 2 * QK_ROPE), row), pl.BlockSpec((tm, QK_ROPE), row),
                  pl.BlockSpec((tm, QK_ROPE), row)],
        out_specs=[pl.BlockSpec((1, group, tm, QK_PAD), out), pl.BlockSpec((1, group, tm, QK_PAD), out),
                   pl.BlockSpec((1, group, tm, V_HEAD), out)],
        out_shape=[jax.ShapeDtypeStruct((batch, heads, seq, QK_PAD), BF16),
                   jax.ShapeDtypeStruct((batch, heads, seq, QK_PAD), BF16),
                   jax.ShapeDtypeStruct((batch, heads, seq, V_HEAD), BF16)],
        compiler_params=_params("parallel", "arbitrary"),
        name="head_proj",
    )(cq, ckv, wq, wkv, kpe, cos, sin)


def _flash_kernel(qi_tab, ki_tab, q_ref, k_ref, v_ref, o_ref, m_ref, l_ref, acc_ref, *, tq, tk, group):
    t = pl.program_id(2)
    qi = qi_tab[t]
    ki = ki_tab[t]

    @pl.when(ki == 0)
    def _():
        m_ref[...] = jnp.full_like(m_ref, NEG_INF)
        l_ref[...] = jnp.zeros_like(l_ref)
        acc_ref[...] = jnp.zeros_like(acc_ref)

    def chain(g, rq, ck, masked):
        s = lax.dot_general(q_ref[0, g, rq, :], k_ref[0, g, ck, :], (((1,), (1,)), ((), ())),
                            preferred_element_type=F32)
        if masked:
            row = qi * tq + rq.start + lax.broadcasted_iota(jnp.int32, s.shape, 0)
            col = ki * tk + ck.start + lax.broadcasted_iota(jnp.int32, s.shape, 1)
            s = jnp.where(row >= col, s, NEG_INF)
        m_prev = m_ref[g, rq, :]
        m_new = jnp.maximum(m_prev, jnp.max(s, axis=-1, keepdims=True))
        alpha = jnp.exp2(m_prev - m_new)
        ps = [jnp.exp2(s[:, j * LANE:(j + 1) * LANE] - m_new) for j in range(s.shape[1] // LANE)]
        l_ref[g, rq, :] = alpha * l_ref[g, rq, :] + functools.reduce(lambda a, b: a + b, ps)
        p = jnp.concatenate(ps, axis=-1).astype(v_ref.dtype)
        acc_ref[g, rq, :] = alpha * acc_ref[g, rq, :] + jnp.dot(p, v_ref[0, g, ck, :],
                                                                preferred_element_type=F32)
        m_ref[g, rq, :] = m_new

    def update(masked):
        for g in range(group):
            if masked and tq == tk and tq % (2 * LANE) == 0:
                half = tq // 2
                chain(g, slice(0, half), slice(0, half), True)
                chain(g, slice(half, tq), slice(0, tk), True)
            else:
                chain(g, slice(0, tq), slice(0, tk), masked)

    crosses_diagonal = (ki + 1) * tk - 1 > qi * tq

    @pl.when(jnp.logical_not(crosses_diagonal))
    def _():
        update(False)

    @pl.when(crosses_diagonal)
    def _():
        update(True)

    @pl.when(ki == ((qi + 1) * tq - 1) // tk)
    def _():
        for g in range(group):
            l = jnp.sum(l_ref[g], axis=-1, keepdims=True)
            o_ref[0, :, g * V_HEAD:(g + 1) * V_HEAD] = (acc_ref[g] / l).astype(o_ref.dtype)


def _flash_call(q, k, v):
    batch, heads, seq, _ = q.shape
    tq = _tile(seq, FLASH_TQ)
    tk = _tile(seq, FLASH_TK)
    group = _tile(heads, FLASH_HEADS_PER_STEP)
    pairs = [(i, j) for i in range(seq // tq) for j in range(((i + 1) * tq - 1) // tk + 1)]
    qi_tab = jnp.asarray([p[0] for p in pairs], jnp.int32)
    ki_tab = jnp.asarray([p[1] for p in pairs], jnp.int32)
    qmap = lambda b, h, t, qt, kt: (b, h, qt[t], 0)
    kmap = lambda b, h, t, qt, kt: (b, h, kt[t], 0)
    return pl.pallas_call(
        functools.partial(_flash_kernel, tq=tq, tk=tk, group=group),
        grid_spec=pltpu.PrefetchScalarGridSpec(
            num_scalar_prefetch=2,
            grid=(batch, heads // group, len(pairs)),
            in_specs=[pl.BlockSpec((1, group, tq, QK_PAD), qmap), pl.BlockSpec((1, group, tk, QK_PAD), kmap),
                      pl.BlockSpec((1, group, tk, V_HEAD), kmap)],
            out_specs=pl.BlockSpec((1, tq, group * V_HEAD), lambda b, h, t, qt, kt: (b, qt[t], h)),
            scratch_shapes=[pltpu.VMEM((group, tq, LANE), F32), pltpu.VMEM((group, tq, LANE), F32),
                            pltpu.VMEM((group, tq, V_HEAD), F32)],
        ),
        out_shape=jax.ShapeDtypeStruct((batch, seq, heads * V_HEAD), BF16),
        compiler_params=_params("parallel", "parallel", "arbitrary"),
        name="flash_attn",
    )(qi_tab, ki_tab, q, k, v)


def _zg_kernel(h_ref, w_ref, b_ref, o_ref, *, n_gelu_blocks):
    j = pl.program_id(0)
    n_sub = 4
    sub = h_ref.shape[0] // n_sub

    def run(epilogue):
        for r in range(n_sub):
            rs = slice(r * sub, (r + 1) * sub)
            z = jnp.dot(h_ref[rs, :], w_ref[...], preferred_element_type=F32)
            o_ref[rs, :] = epilogue(z).astype(o_ref.dtype)

    @pl.when(j < n_gelu_blocks)
    def _():
        run(_gelu_tanh)

    @pl.when(j >= n_gelu_blocks)
    def _():
        run(lambda z: 1.0 / (1.0 + jnp.exp(-(z + b_ref[...]))))


def _zg_call(h, w_zg, bias, n_gelu_cols):
    t, d = h.shape
    n = w_zg.shape[1]
    tm = _tile(t, 1024)
    tn = _tile(n_gelu_cols, 1024)
    return pl.pallas_call(
        functools.partial(_zg_kernel, n_gelu_blocks=n_gelu_cols // tn),
        grid=(n // tn, t // tm),
        in_specs=[pl.BlockSpec((tm, d), lambda j, i: (i, 0)), pl.BlockSpec((d, tn), lambda j, i: (0, j)),
                  pl.BlockSpec((1, tn), lambda j, i: (0, j))],
        out_specs=pl.BlockSpec((tm, tn), lambda j, i: (i, j)),
        out_shape=jax.ShapeDtypeStruct((t, n), BF16),
        compiler_params=_params("parallel", "parallel"),
        name="sgu_gate_proj",
    )(h, w_zg, bias)


def _mix_kernel(u_ref, v_ref, ga_ref, gb_ref, oa_ref, x_ref, lng_ref, lnb_ref, wsp_ref, bsp_ref, wout_ref,
                g2_ref, x1_ref, h2_ref, merged_ref, *, n_groups):
    tm, width = u_ref.shape
    gd = width // n_groups
    v = v_ref[...].astype(F32)
    mu = jnp.mean(v, axis=-1, keepdims=True)
    dv = v - mu
    var = jnp.mean(dv * dv, axis=-1, keepdims=True)
    vn = ((dv * lax.rsqrt(var + EPS)) * lng_ref[...] + lnb_ref[...]).astype(BF16)
    row = lax.broadcasted_iota(jnp.int32, (CHUNK, CHUNK), 0)
    col = lax.broadcasted_iota(jnp.int32, (CHUNK, CHUNK), 1)
    causal = row >= col
    for g in range(n_groups):
        w = jnp.where(causal, wsp_ref[g], 0.0).astype(BF16)
        cs = slice(g * gd, (g + 1) * gd)
        for c in range(tm // CHUNK):
            rs = slice(c * CHUNK, (c + 1) * CHUNK)
            mixed = jnp.dot(w, vn[rs, cs], preferred_element_type=F32) + bsp_ref[:, cs]
            ob = u_ref[rs, cs].astype(F32) * mixed
            merged = ga_ref[rs, cs].astype(F32) * oa_ref[rs, cs].astype(F32) + gb_ref[rs, cs].astype(F32) * ob
            merged_ref[rs, cs] = merged.astype(merged_ref.dtype)
    y = x_ref[...] + jnp.dot(merged_ref[...], wout_ref[...], preferred_element_type=F32)
    x1_ref[...] = y
    h2_ref[...] = _rms(y, g2_ref[...]).astype(h2_ref.dtype)


def _mix_call(zg, oa, x, lng, lnb, wsp, bsp_full, wout, g2):
    t, d = x.shape
    n_groups = wsp.shape[0]
    tm = _tile(t, 256)
    row = lambda i: (i, 0)
    const = lambda i: (0, 0)
    colblk = lambda c: (lambda i: (i, c))
    return pl.pallas_call(
        functools.partial(_mix_kernel, n_groups=n_groups),
        grid=(t // tm,),
        in_specs=[
            pl.BlockSpec((tm, d), colblk(0)), pl.BlockSpec((tm, d), colblk(1)),
            pl.BlockSpec((tm, d), colblk(2)), pl.BlockSpec((tm, d), colblk(3)),
            pl.BlockSpec((tm, d), row), pl.BlockSpec((tm, d), row),
            pl.BlockSpec((1, d), const), pl.BlockSpec((1, d), const),
            pl.BlockSpec(wsp.shape, lambda i: (0, 0, 0)), pl.BlockSpec((CHUNK, d), const),
            pl.BlockSpec((d, d), const), pl.BlockSpec((1, d), const),
        ],
        out_specs=[pl.BlockSpec((tm, d), row), pl.BlockSpec((tm, d), row)],
        out_shape=[jax.ShapeDtypeStruct((t, d), F32), jax.ShapeDtypeStruct((t, d), BF16)],
        scratch_shapes=[pltpu.VMEM((tm, d), BF16)],
        compiler_params=_params("parallel"),
        name="sgu_merge_out",
    )(zg, zg, zg, zg, oa, x, lng, lnb, wsp, bsp_full, wout, g2)


def _top_values(s, k):
    vals = []
    for _ in range(k):
        m = jnp.max(s, axis=0, keepdims=True)
        vals.append(m)
        s = jnp.where(s == m, NEG_INF, s)
    return vals


def _peer_topk_kernel(h2_ref, wpq_ref, keys_ref, lim_ref, s2_ref, p1_ref, p2_ref, *, n_heads):
    tm = h2_ref.shape[0]
    q = jnp.dot(h2_ref[...], wpq_ref[...], preferred_element_type=F32).astype(BF16)
    for h in range(n_heads):
        st = []
        for p in range(2):
            hp = 2 * h + p
            st.append(lax.dot_general(keys_ref[hp], q[:, hp * N_KEYS:(hp + 1) * N_KEYS],
                                      (((1,), (1,)), ((), ())), preferred_element_type=F32))
        v1 = _top_values(st[0], PEER_TOPK)
        v2 = _top_values(st[1], PEER_TOPK)
        v2_all = jnp.concatenate(v2, axis=0)
        v2_top8 = v2_all[:8]
        cand = [v1[0] + v2_all] + [v1[a] + v2_top8 for a in range(1, 8)]
        cand.append(jnp.concatenate(v1[8:], axis=0) + v2[0])
        tops = _top_values(jnp.concatenate(cand, axis=0), PEER_TOPK)
        best = tops[0]
        z = jnp.ones_like(best)
        for r in range(1, PEER_TOPK):
            z = z + jnp.exp(tops[r] - best)
        p1 = jnp.exp(st[0] - v1[0])
        p2 = jnp.exp(st[1] - v2[0]) * (1.0 / z)
        thr = tops[PEER_TOPK - 1]
        lim = jnp.full_like(st[0], jnp.inf)
        for a in range(PEER_TOPK):
            lim_a = jnp.min(jnp.where(v1[a] + v2_all >= thr, v2_all, jnp.inf), axis=0, keepdims=True)
            lim = jnp.where(st[0] == v1[a], lim_a, lim)
        for c in range(tm // LANE):
            ls = slice(c * LANE, (c + 1) * LANE)
            tile = lambda a: a[:, ls].reshape(KEY_TILES, SUBLANE, LANE)
            lim_ref[h, c] = tile(lim)
            s2_ref[h, c] = tile(st[1])
            p1_ref[h, c] = tile(p1)
            p2_ref[h, c] = tile(p2)


def _peer_topk_call(h2, wpq, keys):
    t, d = h2.shape
    n_heads = keys.shape[0] // 2
    tm = _tile(t, 256)
    nc = tm // LANE
    tab = pl.BlockSpec((n_heads, nc, KEY_TILES, SUBLANE, LANE), lambda i: (0, i, 0, 0, 0))
    tab_shape = jax.ShapeDtypeStruct((n_heads, t // LANE, KEY_TILES, SUBLANE, LANE), F32)
    return pl.pallas_call(
        functools.partial(_peer_topk_kernel, n_heads=n_heads),
        grid=(t // tm,),
        in_specs=[pl.BlockSpec((tm, d), lambda i: (i, 0)), pl.BlockSpec(wpq.shape, lambda i: (0, 0)),
                  pl.BlockSpec(keys.shape, lambda i: (0, 0, 0))],
        out_specs=[tab, tab, tab, tab],
        out_shape=[tab_shape, tab_shape, tab_shape, tab_shape],
        compiler_params=_params("parallel"),
        name="peer_topk",
    )(h2, wpq, keys)


def _peer_dense_kernel(h2_ref, u_ref, vt_ref, lim_ref, p1_ref, s2_ref, p2_ref, x1_ref, gf_ref, o_ref,
                       acc_ref, *, n_heads, chains):
    g = pl.program_id(1)
    tm = h2_ref.shape[0]
    te = u_ref.shape[0] // chains
    ni = te // N_KEYS

    @pl.when(g == 0)
    def _():
        acc_ref[...] = jnp.zeros_like(acc_ref)

    h2 = h2_ref[...]
    for ch in range(chains):
        rows = slice(ch * te, (ch + 1) * te)
        st = lax.dot_general(u_ref[rows, :], h2, (((1,), (1,)), ((), ())), preferred_element_type=F32)
        row_blocks = []
        for il in range(ni):
            kr = ch * ni + il
            rs = slice(il * N_KEYS, (il + 1) * N_KEYS)
            lane_blocks = []
            for c in range(tm // LANE):
                ls = slice(c * LANE, (c + 1) * LANE)
                gate = jnp.zeros((KEY_TILES, SUBLANE, LANE), F32)
                for h in range(n_heads):
                    limb = jnp.broadcast_to(lim_ref[h, c, 0, kr:kr + 1, :], (SUBLANE, LANE))
                    p1b = jnp.broadcast_to(p1_ref[h, c, 0, kr:kr + 1, :], (SUBLANE, LANE))
                    sel = s2_ref[h, c] >= limb[None]
                    gate = gate + jnp.where(sel, p1b[None] * p2_ref[h, c], 0.0)
                s = st[rs, ls].reshape(KEY_TILES, SUBLANE, LANE)
                lane_blocks.append((_gelu_tanh(s) * gate).reshape(N_KEYS, LANE).astype(BF16))
            row_blocks.append(jnp.concatenate(lane_blocks, axis=1))
        a = jnp.concatenate(row_blocks, axis=0)
        acc_ref[...] += jnp.dot(vt_ref[:, rows], a, preferred_element_type=F32)

    @pl.when(g == pl.num_programs(1) - 1)
    def _():
        y = x1_ref[...] + acc_ref[...].T
        o_ref[...] = _rms(y, gf_ref[...]).astype(o_ref.dtype)


def _peer_dense_call(h2, u, vt, lim, s2, p1, p2, x1, gf):
    t, d = h2.shape
    n_experts = u.shape[0]
    n_heads = lim.shape[0]
    tm = _tile(t, 512)
    chains = 2
    te = SUBLANE * N_KEYS // chains
    nc = tm // LANE
    n_steps = n_experts // (chains * te)
    tok = lambda i, g: (i, 0)
    once = pl.Buffered(1)
    tab = pl.BlockSpec((n_heads, nc, KEY_TILES, SUBLANE, LANE), lambda i, g: (0, i, 0, 0, 0), pipeline_mode=once)
    key_tile = pl.BlockSpec((n_heads, nc, 1, SUBLANE, LANE), lambda i, g: (0, i, g, 0, 0))
    return pl.pallas_call(
        functools.partial(_peer_dense_kernel, n_heads=n_heads, chains=chains),
        grid=(t // tm, n_steps),
        in_specs=[
            pl.BlockSpec((tm, d), tok),
            pl.BlockSpec((chains * te, d), lambda i, g: (g, 0)),
            pl.BlockSpec((d, chains * te), lambda i, g: (0, g)),
            key_tile, key_tile, tab, tab,
            pl.BlockSpec((tm, d), tok, pipeline_mode=once), pl.BlockSpec((1, d), lambda i, g: (0, 0)),
        ],
        out_specs=pl.BlockSpec((tm, d), tok),
        out_shape=jax.ShapeDtypeStruct((t, d), F32),
        scratch_shapes=[pltpu.VMEM((d, tm), F32)],
        compiler_params=_params("parallel", "arbitrary"),
        name="peer_dense",
    )(h2, u, vt, lim, p1, s2, p2, x1, gf)


def _swap_rope_halves(w):
    half = QK_ROPE // 2
    return jnp.concatenate([w[..., half:], w[..., :half]], axis=-1)


def _layer(x2d, pos, batch, seq, g_norm1, w_in, g_q_a, w_uq, g_kv_a, w_ukv, sgu_ln_g, sgu_ln_b, w_spatial,
           b_spatial, b_gate, w_out, g_norm2, w_peer_q, peer_keys, peer_u, peer_v, g_last):
    t, d = x2d.shape
    q_lora, kv_lora = g_q_a.shape[0], g_kv_a.shape[0]
    heads = w_uq.shape[1]
    off_kpe = q_lora + kv_lora
    off_sgu = off_kpe + QK_ROPE
    row = lambda a: a.reshape(1, -1)

    w_kpe = w_in[:, off_kpe:off_sgu]
    w_lat = jnp.concatenate([w_in[:, :off_sgu], _swap_rope_halves(w_kpe)], axis=1).astype(BF16)
    w_zg = w_in[:, off_sgu:].astype(BF16)
    zg_bias = jnp.concatenate([jnp.zeros((2 * d,), F32), b_gate]).reshape(1, -1)
    wq = jnp.transpose(w_uq, (1, 0, 2))
    wq = jnp.concatenate([wq, _swap_rope_halves(wq[..., QK_NOPE:])], axis=-1).astype(BF16)
    wkv = jnp.transpose(w_ukv, (1, 0, 2)).astype(BF16)
    freqs = ROPE_THETA ** (-jnp.arange(0, QK_ROPE, 2, dtype=F32) / QK_ROPE)
    freq = jnp.concatenate([freqs, freqs]).reshape(1, -1)
    sign = jnp.concatenate([-jnp.ones((QK_ROPE // 2,), F32), jnp.ones((QK_ROPE // 2,), F32)]).reshape(1, -1)
    bsp_full = jnp.repeat(b_spatial.T, d // b_spatial.shape[0], axis=1)
    n_peer_heads = w_peer_q.shape[1]
    wpq = w_peer_q.reshape(d, -1).astype(BF16)
    keys = peer_keys.reshape(2 * n_peer_heads, N_KEYS, -1).astype(BF16)
    u = peer_u.astype(BF16)
    vt = peer_v.T.astype(BF16)

    h, cq, ckv, kpe, cos, sin = _latent_call(x2d, row(g_norm1), w_lat, row(g_q_a), row(g_kv_a), pos, freq, sign)
    scale = (QK_NOPE + QK_ROPE) ** -0.5 * math.log2(math.e)
    q, k, v = _head_proj_calls(cq, ckv, wq, wkv, kpe, cos, sin, batch, seq, scale)
    oa = _flash_call(q, k, v).reshape(t, heads * V_HEAD)
    zg = _zg_call(h, w_zg, zg_bias, 2 * d)
    x1, h2 = _mix_call(zg, oa, x2d, row(sgu_ln_g), row(sgu_ln_b), w_spatial, bsp_full, w_out.astype(BF16),
                       row(g_norm2))
    lim, s2, p1, p2 = _peer_topk_call(h2, wpq, keys)
    return _peer_dense_call(h2, u, vt, lim, s2, p1, p2, x1, row(g_last))


def kernel(x, positions, g_norm1, w_in, g_q_a, w_uq, g_kv_a, w_ukv, sgu_ln_g, sgu_ln_b, w_spatial, b_spatial,
           b_gate, w_out, g_norm2, w_peer_q, peer_keys, peer_u, peer_v, g_final):
    batch, seq, d = x.shape
    depth = g_norm1.shape[0]
    assert depth == 1, "the final norm is fused into the last layer's expert kernel"
    x2d = x.reshape(batch * seq, d)
    pos = positions.astype(F32).reshape(batch * seq, 1)
    out = _layer(x2d, pos, batch, seq, g_norm1[0], w_in[0], g_q_a[0], w_uq[0], g_kv_a[0], w_ukv[0], sgu_ln_g[0],
                 sgu_ln_b[0], w_spatial[0], b_spatial[0], b_gate[0], w_out[0], g_norm2[0], w_peer_q[0],
                 peer_keys[0], peer_u[0], peer_v[0], g_final)
    return out.reshape(batch, seq, d)
```

```python
import functools
import math

import jax
import jax.numpy as jnp
from jax import lax
from jax.experimental import pallas as pl
from jax.experimental.pallas import tpu as pltpu

BF16 = jnp.bfloat16
F32 = jnp.float32

EPS = 1e-6
ROPE_THETA = 10000.0
QK_NOPE = 128
QK_ROPE = 64
V_HEAD = 128
QK_PAD = 256
CHUNK = 128
N_KEYS = 128
PEER_TOPK = 16
LANE = 128
SUBLANE = 8
KEY_TILES = N_KEYS // SUBLANE
VMEM_LIMIT_BYTES = 56 * 1024 * 1024
NEG_INF = float("-inf")
FLASH_TQ = 1024
FLASH_TK = 1024
FLASH_HEADS_PER_STEP = 8


def _params(*sem, flags=None):
    return pltpu.CompilerParams(dimension_semantics=sem, vmem_limit_bytes=VMEM_LIMIT_BYTES, flags=flags)


def _tile(n, pref):
    t = min(n, pref)
    assert n % t == 0, (n, t)
    return t


def _gelu_tanh(x):
    c = math.sqrt(2.0 / math.pi)
    return x * (0.5 * (1.0 + jnp.tanh(c * (x + 0.044715 * (x * x * x)))))


def _rms(x, g):
    r = lax.rsqrt(jnp.mean(x * x, axis=-1, keepdims=True) + EPS)
    return (x * r) * g


def _latent_kernel(x_ref, g1_ref, w_ref, gq_ref, gkv_ref, pos_ref, freq_ref, sign_ref,
                   h_ref, cq_ref, ckv_ref, kpe_ref, cos_ref, sin_ref, *, q_lora, kv_lora):
    h = _rms(x_ref[...], g1_ref[...]).astype(h_ref.dtype)
    h_ref[...] = h
    p = jnp.dot(h, w_ref[...], preferred_element_type=F32)
    o = q_lora + kv_lora
    cq_ref[...] = _rms(p[:, :q_lora], gq_ref[...]).astype(cq_ref.dtype)
    ckv_ref[...] = _rms(p[:, q_lora:o], gkv_ref[...]).astype(ckv_ref.dtype)
    ang = pos_ref[...] * freq_ref[...]
    c = jnp.cos(ang)
    s = jnp.sin(ang) * sign_ref[...]
    cos_ref[...] = c
    sin_ref[...] = s
    kr = p[:, o:o + QK_ROPE] * c + p[:, o + QK_ROPE:o + 2 * QK_ROPE] * s
    kpe_ref[...] = jnp.concatenate([kr, jnp.zeros_like(kr)], axis=-1).astype(kpe_ref.dtype)


def _latent_call(x, g1, w_lat, gq, gkv, pos, freq, sign):
    t, d = x.shape
    q_lora, kv_lora = gq.shape[1], gkv.shape[1]
    n = w_lat.shape[1]
    tm = _tile(t, 512)
    row = lambda i: (i, 0)
    const = lambda i: (0, 0)
    return pl.pallas_call(
        functools.partial(_latent_kernel, q_lora=q_lora, kv_lora=kv_lora),
        grid=(t // tm,),
        in_specs=[
            pl.BlockSpec((tm, d), row), pl.BlockSpec((1, d), const), pl.BlockSpec((d, n), const),
            pl.BlockSpec((1, q_lora), const), pl.BlockSpec((1, kv_lora), const),
            pl.BlockSpec((tm, 1), row), pl.BlockSpec((1, QK_ROPE), const), pl.BlockSpec((1, QK_ROPE), const),
        ],
        out_specs=[
            pl.BlockSpec((tm, d), row), pl.BlockSpec((tm, q_lora), row), pl.BlockSpec((tm, kv_lora), row),
            pl.BlockSpec((tm, 2 * QK_ROPE), row), pl.BlockSpec((tm, QK_ROPE), row), pl.BlockSpec((tm, QK_ROPE), row),
        ],
        out_shape=[
            jax.ShapeDtypeStruct((t, d), BF16),
            jax.ShapeDtypeStruct((t, q_lora), BF16), jax.ShapeDtypeStruct((t, kv_lora), BF16),
            jax.ShapeDtypeStruct((t, 2 * QK_ROPE), BF16),
            jax.ShapeDtypeStruct((t, QK_ROPE), F32), jax.ShapeDtypeStruct((t, QK_ROPE), F32),
        ],
        compiler_params=_params("parallel"),
        name="norm1_latent",
    )(x, g1, w_lat, gq, gkv, pos, freq, sign)


def _head_proj_kernel(cq_ref, ckv_ref, wq_ref, wkv_ref, kpe_ref, cos_ref, sin_ref, q_ref, k_ref, v_ref, *, scale):
    cq = cq_ref[...]
    ckv = ckv_ref[...]
    kpe = kpe_ref[...]
    cos = cos_ref[...]
    sin = sin_ref[...]
    for g in range(wq_ref.shape[0]):
        r = jnp.dot(cq, wq_ref[g], preferred_element_type=F32)
        qr = r[:, QK_NOPE:QK_NOPE + QK_ROPE] * cos + r[:, QK_NOPE + QK_ROPE:] * sin
        q = jnp.concatenate([r[:, :QK_NOPE], qr, jnp.zeros_like(qr)], axis=-1) * scale
        q_ref[0, g] = q.astype(q_ref.dtype)
        r = jnp.dot(ckv, wkv_ref[g], preferred_element_type=F32)
        k_ref[0, g] = jnp.concatenate([r[:, :QK_NOPE].astype(k_ref.dtype), kpe], axis=-1)
        v_ref[0, g] = r[:, QK_NOPE:].astype(v_ref.dtype)


def _head_proj_calls(cq, ckv, wq, wkv, kpe, cos, sin, batch, seq, scale):
    t = cq.shape[0]
    heads = wq.shape[0]
    tm = _tile(seq, 512)
    group = _tile(heads, 8)
    spb = seq // tm
    row = lambda i, h: (i, 0)
    whead = lambda i, h: (h, 0, 0)
    out = lambda i, h: (i // spb, h, i % spb, 0)
    return pl.pallas_call(
        functools.partial(_head_proj_kernel, scale=scale),
        grid=(t // tm, heads // group),
        in_specs=[pl.BlockSpec((tm, cq.shape[1]), row), pl.BlockSpec((tm, ckv.shape[1]), row),
                  pl.BlockSpec((group,) + wq.shape[1:], whead), pl.BlockSpec((group,) + wkv.shape[1:], whead),
                  pl.BlockSpec((tm, 2 * QK_ROPE), row), pl.BlockSpec((tm, QK_ROPE), row),
                  pl.BlockSpec((tm, QK_ROPE), row)],
        out_specs=[pl.BlockSpec((1, group, tm, QK_PAD), out), pl.BlockSpec((1, group, tm, QK_PAD), out),
                   pl.BlockSpec((1, group, tm, V_HEAD), out)],
        out_shape=[jax.ShapeDtypeStruct((batch, heads, seq, QK_PAD), BF16),
                   jax.ShapeDtypeStruct((batch, heads, seq, QK_PAD), BF16),
                   jax.ShapeDtypeStruct((batch, heads, seq, V_HEAD), BF16)],
        compiler_params=_params("parallel", "arbitrary"),
        name="head_proj",
    )(cq, ckv, wq, wkv, kpe, cos, sin)


def _flash_kernel(qi_tab, ki_tab, q_ref, k_ref, v_ref, o_ref, m_ref, l_ref, acc_ref, *, tq, tk, group):
    t = pl.program_id(2)
    qi = qi_tab[t]
    ki = ki_tab[t]

    @pl.when(ki == 0)
    def _():
        m_ref[...] = jnp.full_like(m_ref, NEG_INF)
        l_ref[...] = jnp.zeros_like(l_ref)
        acc_ref[...] = jnp.zeros_like(acc_ref)

    def chain(g, rq, ck, masked):
        s = lax.dot_general(q_ref[0, g, rq, :], k_ref[0, g, ck, :], (((1,), (1,)), ((), ())),
                            preferred_element_type=F32)
        if masked:
            row = qi * tq + rq.start + lax.broadcasted_iota(jnp.int32, s.shape, 0)
            col = ki * tk + ck.start + lax.broadcasted_iota(jnp.int32, s.shape, 1)
            s = jnp.where(row >= col, s, NEG_INF)
        m_prev = m_ref[g, rq, :]
        m_new = jnp.maximum(m_prev, jnp.max(s, axis=-1, keepdims=True))
        alpha = jnp.exp2(m_prev - m_new)
        ps = [jnp.exp2(s[:, j * LANE:(j + 1) * LANE] - m_new) for j in range(s.shape[1] // LANE)]
        l_ref[g, rq, :] = alpha * l_ref[g, rq, :] + functools.reduce(lambda a, b: a + b, ps)
        p = jnp.concatenate(ps, axis=-1).astype(v_ref.dtype)
        acc_ref[g, rq, :] = alpha * acc_ref[g, rq, :] + jnp.dot(p, v_ref[0, g, ck, :],
                                                                preferred_element_type=F32)
        m_ref[g, rq, :] = m_new

    def update(masked):
        for g in range(group):
            if masked and tq == tk and tq % (2 * LANE) == 0:
                half = tq // 2
                chain(g, slice(0, half), slice(0, half), True)
                chain(g, slice(half, tq), slice(0, tk), True)
            else:
                chain(g, slice(0, tq), slice(0, tk), masked)

    crosses_diagonal = (ki + 1) * tk - 1 > qi * tq

    @pl.when(jnp.logical_not(crosses_diagonal))
    def _():
        update(False)

    @pl.when(crosses_diagonal)
    def _():
        update(True)

    @pl.when(ki == ((qi + 1) * tq - 1) // tk)
    def _():
        for g in range(group):
            l = jnp.sum(l_ref[g], axis=-1, keepdims=True)
            o_ref[0, :, g * V_HEAD:(g + 1) * V_HEAD] = (acc_ref[g] / l).astype(o_ref.dtype)


def _flash_call(q, k, v):
    batch, heads, seq, _ = q.shape
    tq = _tile(seq, FLASH_TQ)
    tk = _tile(seq, FLASH_TK)
    group = _tile(heads, FLASH_HEADS_PER_STEP)
    pairs = [(i, j) for i in range(seq // tq) for j in range(((i + 1) * tq - 1) // tk + 1)]
    qi_tab = jnp.asarray([p[0] for p in pairs], jnp.int32)
    ki_tab = jnp.asarray([p[1] for p in pairs], jnp.int32)
    qmap = lambda b, h, t, qt, kt: (b, h, qt[t], 0)
    kmap = lambda b, h, t, qt, kt: (b, h, kt[t], 0)
    return pl.pallas_call(
        functools.partial(_flash_kernel, tq=tq, tk=tk, group=group),
        grid_spec=pltpu.PrefetchScalarGridSpec(
            num_scalar_prefetch=2,
            grid=(batch, heads // group, len(pairs)),
            in_specs=[pl.BlockSpec((1, group, tq, QK_PAD), qmap), pl.BlockSpec((1, group, tk, QK_PAD), kmap),
                      pl.BlockSpec((1, group, tk, V_HEAD), kmap)],
            out_specs=pl.BlockSpec((1, tq, group * V_HEAD), lambda b, h, t, qt, kt: (b, qt[t], h)),
            scratch_shapes=[pltpu.VMEM((group, tq, LANE), F32), pltpu.VMEM((group, tq, LANE), F32),
                            pltpu.VMEM((group, tq, V_HEAD), F32)],
        ),
        out_shape=jax.ShapeDtypeStruct((batch, seq, heads * V_HEAD), BF16),
        compiler_params=_params("parallel", "parallel", "arbitrary"),
        name="flash_attn",
    )(qi_tab, ki_tab, q, k, v)


def _zg_kernel(h_ref, w_ref, b_ref, o_ref, *, n_gelu_blocks):
    j = pl.program_id(0)
    n_sub = 4
    sub = h_ref.shape[0] // n_sub

    def run(epilogue):
        for r in range(n_sub):
            rs = slice(r * sub, (r + 1) * sub)
            z = jnp.dot(h_ref[rs, :], w_ref[...], preferred_element_type=F32)
            o_ref[rs, :] = epilogue(z).astype(o_ref.dtype)

    @pl.when(j < n_gelu_blocks)
    def _():
        run(_gelu_tanh)

    @pl.when(j >= n_gelu_blocks)
    def _():
        run(lambda z: 1.0 / (1.0 + jnp.exp(-(z + b_ref[...]))))


def _zg_call(h, w_zg, bias, n_gelu_cols):
    t, d = h.shape
    n = w_zg.shape[1]
    tm = _tile(t, 1024)
    tn = _tile(n_gelu_cols, 1024)
    return pl.pallas_call(
        functools.partial(_zg_kernel, n_gelu_blocks=n_gelu_cols // tn),
        grid=(n // tn, t // tm),
        in_specs=[pl.BlockSpec((tm, d), lambda j, i: (i, 0)), pl.BlockSpec((d, tn), lambda j, i: (0, j)),
                  pl.BlockSpec((1, tn), lambda j, i: (0, j))],
        out_specs=pl.BlockSpec((tm, tn), lambda j, i: (i, j)),
        out_shape=jax.ShapeDtypeStruct((t, n), BF16),
        compiler_params=_params("parallel", "parallel"),
        name="sgu_gate_proj",
    )(h, w_zg, bias)


def _mix_kernel(u_ref, v_ref, ga_ref, gb_ref, oa_ref, x_ref, lng_ref, lnb_ref, wsp_ref, bsp_ref, wout_ref,
                g2_ref, x1_ref, h2_ref, merged_ref, *, n_groups):
    tm, width = u_ref.shape
    gd = width // n_groups
    v = v_ref[...].astype(F32)
    mu = jnp.mean(v, axis=-1, keepdims=True)
    dv = v - mu
    var = jnp.mean(dv * dv, axis=-1, keepdims=True)
    vn = ((dv * lax.rsqrt(var + EPS)) * lng_ref[...] + lnb_ref[...]).astype(BF16)
    row = lax.broadcasted_iota(jnp.int32, (CHUNK, CHUNK), 0)
    col = lax.broadcasted_iota(jnp.int32, (CHUNK, CHUNK), 1)
    causal = row >= col
    for g in range(n_groups):
        w = jnp.where(causal, wsp_ref[g], 0.0).astype(BF16)
        cs = slice(g * gd, (g + 1) * gd)
        for c in range(tm // CHUNK):
            rs = slice(c * CHUNK, (c + 1) * CHUNK)
            mixed = jnp.dot(w, vn[rs, cs], preferred_element_type=F32) + bsp_ref[:, cs]
            ob = u_ref[rs, cs].astype(F32) * mixed
            merged = ga_ref[rs, cs].astype(F32) * oa_ref[rs, cs].astype(F32) + gb_ref[rs, cs].astype(F32) * ob
            merged_ref[rs, cs] = merged.astype(merged_ref.dtype)
    y = x_ref[...] + jnp.dot(merged_ref[...], wout_ref[...], preferred_element_type=F32)
    x1_ref[...] = y
    h2_ref[...] = _rms(y, g2_ref[...]).astype(h2_ref.dtype)


def _mix_call(zg, oa, x, lng, lnb, wsp, bsp_full, wout, g2):
    t, d = x.shape
    n_groups = wsp.shape[0]
    tm = _tile(t, 256)
    row = lambda i: (i, 0)
    const = lambda i: (0, 0)
    colblk = lambda c: (lambda i: (i, c))
    return pl.pallas_call(
        functools.partial(_mix_kernel, n_groups=n_groups),
        grid=(t // tm,),
        in_specs=[
            pl.BlockSpec((tm, d), colblk(0)), pl.BlockSpec((tm, d), colblk(1)),
            pl.BlockSpec((tm, d), colblk(2)), pl.BlockSpec((tm, d), colblk(3)),
            pl.BlockSpec((tm, d), row), pl.BlockSpec((tm, d), row),
            pl.BlockSpec((1, d), const), pl.BlockSpec((1, d), const),
            pl.BlockSpec(wsp.shape, lambda i: (0, 0, 0)), pl.BlockSpec((CHUNK, d), const),
            pl.BlockSpec((d, d), const), pl.BlockSpec((1, d), const),
        ],
        out_specs=[pl.BlockSpec((tm, d), row), pl.BlockSpec((tm, d), row)],
        out_shape=[jax.ShapeDtypeStruct((t, d), F32), jax.ShapeDtypeStruct((t, d), BF16)],
        scratch_shapes=[pltpu.VMEM((tm, d), BF16)],
        compiler_params=_params("parallel"),
        name="sgu_merge_out",
    )(zg, zg, zg, zg, oa, x, lng, lnb, wsp, bsp_full, wout, g2)


def _top_values(s, k):
    vals = []
    for _ in range(k):
        m = jnp.max(s, axis=0, keepdims=True)
        vals.append(m)
        s = jnp.where(s == m, NEG_INF, s)
    return vals


def _peer_topk_kernel(h2_ref, wpq_ref, keys_ref, lim_ref, s2_ref, p1_ref, p2_ref, *, n_heads):
    tm = h2_ref.shape[0]
    q = jnp.dot(h2_ref[...], wpq_ref[...], preferred_element_type=F32).astype(BF16)
    for h in range(n_heads):
        st = []
        for p in range(2):
            hp = 2 * h + p
            st.append(lax.dot_general(keys_ref[hp], q[:, hp * N_KEYS:(hp + 1) * N_KEYS],
                                      (((1,), (1,)), ((), ())), preferred_element_type=F32))
        v1 = _top_values(st[0], PEER_TOPK)
        v2 = _top_values(st[1], PEER_TOPK)
        v2_all = jnp.concatenate(v2, axis=0)
        v2_top8 = v2_all[:8]
        cand = [v1[0] + v2_all] + [v1[a] + v2_top8 for a in range(1, 8)]
        cand.append(jnp.concatenate(v1[8:], axis=0) + v2[0])
        tops = _top_values(jnp.concatenate(cand, axis=0), PEER_TOPK)
        best = tops[0]
        z = jnp.ones_like(best)
        for r in range(1, PEER_TOPK):
            z = z + jnp.exp(tops[r] - best)
        p1 = jnp.exp(st[0] - v1[0])
        p2 = jnp.exp(st[1] - v2[0]) * (1.0 / z)
        thr = tops[PEER_TOPK - 1]
        lim = jnp.full_like(st[0], jnp.inf)
        for a in range(PEER_TOPK):
            lim_a = jnp.min(jnp.where(v1[a] + v2_all >= thr, v2_all, jnp.inf), axis=0, keepdims=True)
            lim = jnp.where(st[0] == v1[a], lim_a, lim)
        for c in range(tm // LANE):
            ls = slice(c * LANE, (c + 1) * LANE)
            tile = lambda a: a[:, ls].reshape(KEY_TILES, SUBLANE, LANE)
            lim_ref[h, c] = tile(lim)
            s2_ref[h, c] = tile(st[1])
            p1_ref[h, c] = tile(p1)
            p2_ref[h, c] = tile(p2)


def _peer_topk_call(h2, wpq, keys):
    t, d = h2.shape
    n_heads = keys.shape[0] // 2
    tm = _tile(t, 256)
    nc = tm // LANE
    tab = pl.BlockSpec((n_heads, nc, KEY_TILES, SUBLANE, LANE), lambda i: (0, i, 0, 0, 0))
    tab_shape = jax.ShapeDtypeStruct((n_heads, t // LANE, KEY_TILES, SUBLANE, LANE), F32)
    return pl.pallas_call(
        functools.partial(_peer_topk_kernel, n_heads=n_heads),
        grid=(t // tm,),
        in_specs=[pl.BlockSpec((tm, d), lambda i: (i, 0)), pl.BlockSpec(wpq.shape, lambda i: (0, 0)),
                  pl.BlockSpec(keys.shape, lambda i: (0, 0, 0))],
        out_specs=[tab, tab, tab, tab],
        out_shape=[tab_shape, tab_shape, tab_shape, tab_shape],
        compiler_params=_params("parallel"),
        name="peer_topk",
    )(h2, wpq, keys)


def _peer_dense_kernel(h2_ref, u_ref, vt_ref, lim_ref, p1_ref, s2_ref, p2_ref, x1_ref, gf_ref, o_ref,
                       acc_ref, *, n_heads, chains):
    g = pl.program_id(1)
    tm = h2_ref.shape[0]
    te = u_ref.shape[0] // chains
    ni = te // N_KEYS

    @pl.when(g == 0)
    def _():
        acc_ref[...] = jnp.zeros_like(acc_ref)

    h2 = h2_ref[...]
    for ch in range(chains):
        rows = slice(ch * te, (ch + 1) * te)
        st = lax.dot_general(u_ref[rows, :], h2, (((1,), (1,)), ((), ())), preferred_element_type=F32)
        row_blocks = []
        for il in range(ni):
            kr = ch * ni + il
            rs = slice(il * N_KEYS, (il + 1) * N_KEYS)
            lane_blocks = []
            for c in range(tm // LANE):
                ls = slice(c * LANE, (c + 1) * LANE)
                gate = jnp.zeros((KEY_TILES, SUBLANE, LANE), F32)
                for h in range(n_heads):
                    limb = jnp.broadcast_to(lim_ref[h, c, 0, kr:kr + 1, :], (SUBLANE, LANE))
                    p1b = jnp.broadcast_to(p1_ref[h, c, 0, kr:kr + 1, :], (SUBLANE, LANE))
                    sel = s2_ref[h, c] >= limb[None]
                    gate = gate + jnp.where(sel, p1b[None] * p2_ref[h, c], 0.0)
                s = st[rs, ls].reshape(KEY_TILES, SUBLANE, LANE)
                lane_blocks.append((_gelu_tanh(s) * gate).reshape(N_KEYS, LANE).astype(BF16))
            row_blocks.append(jnp.concatenate(lane_blocks, axis=1))
        a = jnp.concatenate(row_blocks, axis=0)
        acc_ref[...] += jnp.dot(vt_ref[:, rows], a, preferred_element_type=F32)

    @pl.when(g == pl.num_programs(1) - 1)
    def _():
        y = x1_ref[...] + acc_ref[...].T
        o_ref[...] = _rms(y, gf_ref[...]).astype(o_ref.dtype)


def _peer_dense_call(h2, u, vt, lim, s2, p1, p2, x1, gf):
    t, d = h2.shape
    n_experts = u.shape[0]
    n_heads = lim.shape[0]
    tm = _tile(t, 512)
    chains = 2
    te = SUBLANE * N_KEYS // chains
    nc = tm // LANE
    n_steps = n_experts // (chains * te)
    tok = lambda i, g: (i, 0)
    once = pl.Buffered(1)
    tab = pl.BlockSpec((n_heads, nc, KEY_TILES, SUBLANE, LANE), lambda i, g: (0, i, 0, 0, 0), pipeline_mode=once)
    key_tile = pl.BlockSpec((n_heads, nc, 1, SUBLANE, LANE), lambda i, g: (0, i, g, 0, 0))
    return pl.pallas_call(
        functools.partial(_peer_dense_kernel, n_heads=n_heads, chains=chains),
        grid=(t // tm, n_steps),
        in_specs=[
            pl.BlockSpec((tm, d), tok),
            pl.BlockSpec((chains * te, d), lambda i, g: (g, 0)),
            pl.BlockSpec((d, chains * te), lambda i, g: (0, g)),
            key_tile, key_tile, tab, tab,
            pl.BlockSpec((tm, d), tok, pipeline_mode=once), pl.BlockSpec((1, d), lambda i, g: (0, 0)),
        ],
        out_specs=pl.BlockSpec((tm, d), tok),
        out_shape=jax.ShapeDtypeStruct((t, d), F32),
        scratch_shapes=[pltpu.VMEM((d, tm), F32)],
        compiler_params=_params("parallel", "arbitrary"),
        name="peer_dense",
    )(h2, u, vt, lim, p1, s2, p2, x1, gf)


def _swap_rope_halves(w):
    half = QK_ROPE // 2
    return jnp.concatenate([w[..., half:], w[..., :half]], axis=-1)


def _layer(x2d, pos, batch, seq, g_norm1, w_in, g_q_a, w_uq, g_kv_a, w_ukv, sgu_ln_g, sgu_ln_b, w_spatial,
           b_spatial, b_gate, w_out, g_norm2, w_peer_q, peer_keys, peer_u, peer_v, g_last):
    t, d = x2d.shape
    q_lora, kv_lora = g_q_a.shape[0], g_kv_a.shape[0]
    heads = w_uq.shape[1]
    off_kpe = q_lora + kv_lora
    off_sgu = off_kpe + QK_ROPE
    row = lambda a: a.reshape(1, -1)

    w_kpe = w_in[:, off_kpe:off_sgu]
    w_lat = jnp.concatenate([w_in[:, :off_sgu], _swap_rope_halves(w_kpe)], axis=1).astype(BF16)
    w_zg = w_in[:, off_sgu:].astype(BF16)
    zg_bias = jnp.concatenate([jnp.zeros((2 * d,), F32), b_gate]).reshape(1, -1)
    wq = jnp.transpose(w_uq, (1, 0, 2))
    wq = jnp.concatenate([wq, _swap_rope_halves(wq[..., QK_NOPE:])], axis=-1).astype(BF16)
    wkv = jnp.transpose(w_ukv, (1, 0, 2)).astype(BF16)
    freqs = ROPE_THETA ** (-jnp.arange(0, QK_ROPE, 2, dtype=F32) / QK_ROPE)
    freq = jnp.concatenate([freqs, freqs]).reshape(1, -1)
    sign = jnp.concatenate([-jnp.ones((QK_ROPE // 2,), F32), jnp.ones((QK_ROPE // 2,), F32)]).reshape(1, -1)
    bsp_full = jnp.repeat(b_spatial.T, d // b_spatial.shape[0], axis=1)
    n_peer_heads = w_peer_q.shape[1]
    wpq = w_peer_q.reshape(d, -1).astype(BF16)
    keys = peer_keys.reshape(2 * n_peer_heads, N_KEYS, -1).astype(BF16)
    u = peer_u.astype(BF16)
    vt = peer_v.T.astype(BF16)

    h, cq, ckv, kpe, cos, sin = _latent_call(x2d, row(g_norm1), w_lat, row(g_q_a), row(g_kv_a), pos, freq, sign)
    scale = (QK_NOPE + QK_ROPE) ** -0.5 * math.log2(math.e)
    q, k, v = _head_proj_calls(cq, ckv, wq, wkv, kpe, cos, sin, batch, seq, scale)
    oa = _flash_call(q, k, v).reshape(t, heads * V_HEAD)
    zg = _zg_call(h, w_zg, zg_bias, 2 * d)
    x1, h2 = _mix_call(zg, oa, x2d, row(sgu_ln_g), row(sgu_ln_b), w_spatial, bsp_full, w_out.astype(BF16),
                       row(g_norm2))
    lim, s2, p1, p2 = _peer_topk_call(h2, wpq, keys)
    return _peer_dense_call(h2, u, vt, lim, s2, p1, p2, x1, row(g_last))


def kernel(x, positions, g_norm1, w_in, g_q_a, w_uq, g_kv_a, w_ukv, sgu_ln_g, sgu_ln_b, w_spatial, b_spatial,
           b_gate, w_out, g_norm2, w_peer_q, peer_keys, peer_u, peer_v, g_final):
    batch, seq, d = x.shape
    depth = g_norm1.shape[0]
    assert depth == 1, "the final norm is fused into the last layer's expert kernel"
    x2d = x.reshape(batch * seq, d)
    pos = positions.astype(F32).reshape(batch * seq, 1)
    out = _layer(x2d, pos, batch, seq, g_norm1[0], w_in[0], g_q_a[0], w_uq[0], g_kv_a[0], w_ukv[0], sgu_ln_g[0],
                 sgu_ln_b[0], w_spatial[0], b_spatial[0], b_gate[0], w_out[0], g_norm2[0], w_peer_q[0],
                 peer_keys[0], peer_u[0], peer_v[0], g_final)
    return out.reshape(batch, seq, d)
```
